```python
import jax, jax.numpy as jnp
from jax import lax
import numpy as np

D_MODEL = 1024
BATCH = 2
SEQ = 8192
DEPTH = 2

N_MIXERS = 2
S5_GROUP = 16
S5_GROUPS = D_MODEL // S5_GROUP
S5_STATE = 64
S5_CHUNK = 128
DT_MIN = 1e-3
DT_MAX = 1e-1
FOX_HEADS = 16
FOX_HEAD_DIM = D_MODEL // FOX_HEADS
Q_BLOCK = 128
FGATE_B_LO = 2.0
FGATE_B_HI = 6.0
D_FF = 2816
CONV_W = 3
EPS = 1e-6
N_S5 = (DEPTH + 1) // 2
N_FOX = DEPTH // 2

kernel_name = "hybrid_s5_fox_convffn_adaln"


def rmsnorm(x, g):
    x32 = x.astype(jnp.float32)
    y = x32 * lax.rsqrt(jnp.mean(x32 * x32, axis=-1, keepdims=True) + EPS)
    return (y * g.astype(jnp.float32)).astype(x.dtype)


def adaln_params(c, w, b):
    mod = jax.nn.silu(c) @ w + b
    shift, scale, gate = jnp.split(mod, 3, axis=-1)
    return shift[:, None, :], scale[:, None, :], gate[:, None, :]


def modulate(x, g, shift, scale):
    return rmsnorm(x, g) * (1.0 + scale) + shift


def s5_mixer(h, w_in, lam_re, lam_im, log_dt, b_re, b_im, c_re, c_im, d_skip, w_glu, w_out):
    f32 = jnp.float32
    Bsz, S, _ = h.shape
    u = h @ w_in
    ug = u.reshape(Bsz, S, S5_GROUPS, S5_GROUP)
    lre = lam_re.astype(f32)
    lim = lam_im.astype(f32)
    dt = jnp.exp(log_dt.astype(f32))[:, None]
    mag = jnp.exp(lre * dt)
    lb_re = mag * jnp.cos(lim * dt)
    lb_im = mag * jnp.sin(lim * dt)
    num_re = lb_re - 1.0
    den = lre * lre + lim * lim
    k_re = (num_re * lre + lb_im * lim) / den
    k_im = (lb_im * lre - num_re * lim) / den
    br = b_re.astype(f32)
    bi = b_im.astype(f32)
    bb_re = k_re[..., None] * br - k_im[..., None] * bi
    bb_im = k_re[..., None] * bi + k_im[..., None] * br
    cr = c_re.astype(f32)
    ci = c_im.astype(f32)

    def combine(left, right):
        a1r, a1i, b1r, b1i = left
        a2r, a2i, b2r, b2i = right
        ar = a2r * a1r - a2i * a1i
        ai = a2r * a1i + a2i * a1r
        b_r = a2r * b1r - a2i * b1i + b2r
        b_i = a2r * b1i + a2i * b1r + b2i
        return ar, ai, b_r, b_i

    def chunk_step(carry, u_c):
        h_re, h_im = carry
        bu_re = jnp.einsum('blgc,gpc->blgp', u_c, bb_re)
        bu_im = jnp.einsum('blgc,gpc->blgp', u_c, bb_im)
        bu_re = bu_re.at[:, 0].add(lb_re * h_re - lb_im * h_im)
        bu_im = bu_im.at[:, 0].add(lb_re * h_im + lb_im * h_re)
        a_re = jnp.broadcast_to(lb_re, bu_re.shape)
        a_im = jnp.broadcast_to(lb_im, bu_im.shape)
        _, _, hs_re, hs_im = lax.associative_scan(combine, (a_re, a_im, bu_re, bu_im), axis=1)
        y = jnp.einsum('blgp,gcp->blgc', hs_re, cr) - jnp.einsum('blgp,gcp->blgc', hs_im, ci)
        return (hs_re[:, -1], hs_im[:, -1]), y

    n_chunks = S // S5_CHUNK
    u_chunks = ug.reshape(Bsz, n_chunks, S5_CHUNK, S5_GROUPS, S5_GROUP).transpose(1, 0, 2, 3, 4)
    h0 = (jnp.zeros((Bsz, S5_GROUPS, S5_STATE), f32), jnp.zeros((Bsz, S5_GROUPS, S5_STATE), f32))
    _, ys = lax.scan(chunk_step, h0, u_chunks)
    y = ys.transpose(1, 0, 2, 3, 4).reshape(Bsz, S, D_MODEL)
    y = y + d_skip.astype(f32) * u.astype(f32)
    y = jax.nn.gelu(y)
    y = y * jax.nn.sigmoid(y @ w_glu.astype(f32))
    return (y @ w_out.astype(f32)).astype(h.dtype)


def fox_mixer(h, w_in, b_f, w_out):
    f32 = jnp.float32
    Bsz, S, _ = h.shape
    proj = h @ w_in
    q = proj[..., :D_MODEL].reshape(Bsz, S, FOX_HEADS, FOX_HEAD_DIM) * (FOX_HEAD_DIM ** -0.5)
    k = proj[..., D_MODEL:2 * D_MODEL].reshape(Bsz, S, FOX_HEADS, FOX_HEAD_DIM)
    v = proj[..., 2 * D_MODEL:3 * D_MODEL].reshape(Bsz, S, FOX_HEADS, FOX_HEAD_DIM)
    f_logit = proj[..., 3 * D_MODEL:]
    log_f = jax.nn.log_sigmoid((f_logit + b_f).astype(f32))
    F = lax.cumsum(log_f, axis=1).transpose(0, 2, 1)
    n_q = S // Q_BLOCK
    q_blocks = q.reshape(Bsz, n_q, Q_BLOCK, FOX_HEADS, FOX_HEAD_DIM).transpose(1, 0, 3, 2, 4)
    F_blocks = F.reshape(Bsz, FOX_HEADS, n_q, Q_BLOCK).transpose(2, 0, 1, 3)
    k_pos = jnp.arange(S)

    def attend_block(args):
        qi, q_blk, Fq = args
        s = jnp.einsum('bhqd,bshd->bhqs', q_blk, k).astype(f32)
        s = s + Fq[..., None] - F[:, :, None, :]
        q_pos = qi * Q_BLOCK + jnp.arange(Q_BLOCK)
        causal = k_pos[None, :] <= q_pos[:, None]
        s = jnp.where(causal, s, -jnp.inf)
        p = jax.nn.softmax(s, axis=-1)
        return jnp.einsum('bhqs,bshd->bqhd', p.astype(v.dtype), v)

    o = lax.map(attend_block, (jnp.arange(n_q), q_blocks, F_blocks))
    o = o.transpose(1, 0, 2, 3, 4).reshape(Bsz, S, D_MODEL)
    return o @ w_out


def conv_ffn(h, w_up, conv_w, conv_b, w_down):
    up = h @ w_up
    a, b = jnp.split(up, 2, axis=-1)
    S = a.shape[1]
    ap = jnp.pad(a, ((0, 0), (CONV_W - 1, 0), (0, 0)))
    a_conv = conv_b
    for i in range(CONV_W):
        a_conv = a_conv + ap[:, i:i + S] * conv_w[i]
    return (jax.nn.silu(a_conv) * b) @ w_down


def setup_inputs(seed: int = 0) -> dict:
    key = jax.random.key(seed)
    ks = jax.random.split(key, 32)
    f32 = jnp.float32
    D, G, P, Cg, H, F = D_MODEL, S5_GROUPS, S5_STATE, S5_GROUP, FOX_HEADS, D_FF
    nrm = lambda k, shape, s: jax.random.normal(k, shape, f32) * s
    x = nrm(ks[0], (BATCH, SEQ, D), 1.0)
    c = nrm(ks[1], (BATCH, D), 1.0)
    norm_g = 1.0 + nrm(ks[2], (DEPTH, 2, D), 0.01)
    ada_w = nrm(ks[3], (DEPTH, 2, D, 3 * D), 0.5 * D ** -0.5)
    ada_b = nrm(ks[4], (DEPTH, 2, 3 * D), 0.01)
    s5_w_in = nrm(ks[5], (N_S5, D, D), D ** -0.5)
    s5_lam_re = -0.5 + nrm(ks[6], (N_S5, G, P), 0.01)
    s5_lam_im = jnp.pi * jnp.arange(P, dtype=f32) + nrm(ks[7], (N_S5, G, P), 0.01)
    s5_log_dt = jax.random.uniform(ks[8], (N_S5, G), f32, np.log(DT_MIN), np.log(DT_MAX))
    s5_b_re = nrm(ks[9], (N_S5, G, P, Cg), (2.0 * Cg) ** -0.5)
    s5_b_im = nrm(ks[10], (N_S5, G, P, Cg), (2.0 * Cg) ** -0.5)
    s5_c_re = nrm(ks[11], (N_S5, G, Cg, P), (2.0 * P) ** -0.5)
    s5_c_im = nrm(ks[12], (N_S5, G, Cg, P), (2.0 * P) ** -0.5)
    s5_d = nrm(ks[13], (N_S5, D), 1.0)
    s5_w_glu = nrm(ks[14], (N_S5, D, D), D ** -0.5)
    s5_w_out = nrm(ks[15], (N_S5, D, D), D ** -0.5)
    fox_w_in = nrm(ks[16], (N_FOX, D, 3 * D + H), D ** -0.5)
    fox_b_f = jax.random.uniform(ks[17], (N_FOX, H), f32, FGATE_B_LO, FGATE_B_HI)
    fox_w_out = nrm(ks[18], (N_FOX, D, D), D ** -0.5)
    ffn_w_up = nrm(ks[19], (DEPTH, D, 2 * F), D ** -0.5)
    ffn_conv_w = nrm(ks[20], (DEPTH, CONV_W, F), CONV_W ** -0.5)
    ffn_conv_b = nrm(ks[21], (DEPTH, F), 0.01)
    ffn_w_down = nrm(ks[22], (DEPTH, F, D), F ** -0.5)
    final_g = 1.0 + nrm(ks[23], (D,), 0.01)
    return {"x": x, "c": c, "norm_g": norm_g, "ada_w": ada_w, "ada_b": ada_b,
            "s5_w_in": s5_w_in, "s5_lam_re": s5_lam_re, "s5_lam_im": s5_lam_im,
            "s5_log_dt": s5_log_dt, "s5_b_re": s5_b_re, "s5_b_im": s5_b_im,
            "s5_c_re": s5_c_re, "s5_c_im": s5_c_im, "s5_d": s5_d,
            "s5_w_glu": s5_w_glu, "s5_w_out": s5_w_out,
            "fox_w_in": fox_w_in, "fox_b_f": fox_b_f, "fox_w_out": fox_w_out,
            "ffn_w_up": ffn_w_up, "ffn_conv_w": ffn_conv_w, "ffn_conv_b": ffn_conv_b,
            "ffn_w_down": ffn_w_down, "final_g": final_g}


def reference(x, c, norm_g, ada_w, ada_b,
              s5_w_in, s5_lam_re, s5_lam_im, s5_log_dt, s5_b_re, s5_b_im,
              s5_c_re, s5_c_im, s5_d, s5_w_glu, s5_w_out,
              fox_w_in, fox_b_f, fox_w_out,
              ffn_w_up, ffn_conv_w, ffn_conv_b, ffn_w_down, final_g):
    h = x
    for i in range(DEPTH):
        j = i // N_MIXERS
        shift, scale, gate = adaln_params(c, ada_w[i, 0], ada_b[i, 0])
        hn = modulate(h, norm_g[i, 0], shift, scale)
        if i % N_MIXERS == 0:
            m = s5_mixer(hn, s5_w_in[j], s5_lam_re[j], s5_lam_im[j], s5_log_dt[j],
                         s5_b_re[j], s5_b_im[j], s5_c_re[j], s5_c_im[j], s5_d[j],
                         s5_w_glu[j], s5_w_out[j])
        else:
            m = fox_mixer(hn, fox_w_in[j], fox_b_f[j], fox_w_out[j])
        h = (h + gate * m).astype(x.dtype)
        shift, scale, gate = adaln_params(c, ada_w[i, 1], ada_b[i, 1])
        hn = modulate(h, norm_g[i, 1], shift, scale)
        f = conv_ffn(hn, ffn_w_up[i], ffn_conv_w[i], ffn_conv_b[i], ffn_w_down[i])
        h = (h + gate * f).astype(x.dtype)
    return rmsnorm(h, final_g)
```

```python
import functools

import jax
import jax.numpy as jnp
from jax import lax
from jax.experimental import pallas as pl
from jax.experimental.pallas import tpu as pltpu

EPS = 1e-6
LANES = 128
SUBLANES = 8
BF16_ROWS = 16
VMEM_LIMIT = 56 * 1024 * 1024

S5_CHUNK = 4
S5_SEGMENTS = SUBLANES

f32 = jnp.float32
bf16 = jnp.bfloat16


def _cparams(*sem):
    return pltpu.CompilerParams(dimension_semantics=sem, vmem_limit_bytes=VMEM_LIMIT)


def _resident(shape):
    nd = len(shape)
    return pl.BlockSpec(shape, lambda *_: (0,) * nd, pipeline_mode=pl.Buffered(1))


def _dot(a, b):
    return jnp.dot(a, b, preferred_element_type=f32)


def _dot_nt(a, b):
    return lax.dot_general(a, b, (((1,), (1,)), ((), ())), preferred_element_type=f32)


def _split3(x):
    hi = x.astype(bf16)
    r1 = x - hi.astype(f32)
    mid = r1.astype(bf16)
    lo = (r1 - mid.astype(f32)).astype(bf16)
    return hi, mid, lo


def _modulate(x, g, shift, scale):
    ms = jnp.mean(x * x, axis=-1, keepdims=True)
    return x * lax.rsqrt(ms + EPS) * g * (1.0 + scale) + shift


def _silu(x):
    return x * (1.0 / (1.0 + jnp.exp(-x)))


def _gelu_tanh(x):
    c = 0.7978845608028654
    return 0.5 * x * (1.0 + jnp.tanh(c * (x + 0.044715 * (x * x * x))))


def _adaln_kernel(ct_ref, w_ref, b_ref, o_ref):
    ct = ct_ref[...]
    cs = _silu(ct)
    w = w_ref[...]
    for b in range(ct.shape[1]):
        col = cs[:, b:b + 1]
        o_ref[b:b + 1, :] = jnp.sum(w * col, axis=0, keepdims=True) + b_ref[...]


def _adaln(c, ada_w, ada_b):
    B, D = c.shape
    L = ada_w.shape[0] * ada_w.shape[1]
    N = ada_w.shape[-1]
    tn = 512
    w = ada_w.reshape(L, D, N)
    bias = ada_b.reshape(L, 1, N)
    return pl.pallas_call(
        _adaln_kernel,
        grid=(L, N // tn),
        in_specs=[
            pl.BlockSpec((D, B), lambda l, j: (0, 0)),
            pl.BlockSpec((None, D, tn), lambda l, j: (l, 0, j)),
            pl.BlockSpec((None, 1, tn), lambda l, j: (l, 0, j)),
        ],
        out_specs=pl.BlockSpec((None, B, tn), lambda l, j: (l, 0, j)),
        out_shape=jax.ShapeDtypeStruct((L, B, N), f32),
        compiler_params=_cparams("parallel", "parallel"),
        name="adaln",
    )(c.T, w, bias)


def _mod_spec(D):
    return lambda l: pl.BlockSpec((None, None, 3, D), lambda b, i: (l, b, 0, 0))


def _s5_in_kernel(x_ref, mod_ref, g_ref, w_ref, u_ref):
    x = x_ref[...]
    hn = _modulate(x, g_ref[...], mod_ref[0:1, :], mod_ref[1:2, :])
    u_ref[...] = _dot(hn.astype(bf16), w_ref[...])


def _s5_in(x, mod4, l, g, w_bf, tm):
    B, S, D = x.shape
    return pl.pallas_call(
        _s5_in_kernel,
        grid=(B, S // tm),
        in_specs=[
            pl.BlockSpec((None, tm, D), lambda b, i: (b, i, 0)),
            _mod_spec(D)(l),
            _resident((1, D)),
            _resident((D, D)),
        ],
        out_specs=pl.BlockSpec((None, tm, D), lambda b, i: (b, i, 0)),
        out_shape=jax.ShapeDtypeStruct((B, S, D), f32),
        compiler_params=_cparams("parallel", "parallel"),
        name="s5_in",
    )(x, mod4, g, w_bf)


def _s5_weights_kernel(lre_ref, lim_ref, ldt_ref, br_ref, bi_ref, cr_ref, ci_ref,
                       wcs_ref, wout_ref, wtz_ref, adec_ref, *, lc, cg, p_state):
    lre = lre_ref[...]
    lim = lim_ref[...]
    dt = jnp.exp(ldt_ref[...])
    nl = lre.shape[1]
    mag = jnp.exp(lre * dt)
    lam_r = mag * jnp.cos(lim * dt)
    lam_i = mag * jnp.sin(lim * dt)
    num_r = lam_r - 1.0
    den = lre * lre + lim * lim
    k_r = (num_r * lre + lam_i * lim) / den
    k_i = (lam_i * lre - num_r * lim) / den
    br = br_ref[...]
    bi = bi_ref[...]
    bb_r = k_r * br - k_i * bi
    bb_i = k_r * bi + k_i * br
    cr = cr_ref[...]
    ci = ci_ref[...]

    ngrp = nl // p_state
    rows = ngrp * cg
    row_grp = lax.broadcasted_iota(jnp.int32, (rows, nl), 0) // cg
    lane_grp = lax.broadcasted_iota(jnp.int32, (rows, nl), 1) // p_state
    own = row_grp == lane_grp

    def slab(xr, xi):
        tr = jnp.where(own, jnp.tile(xr, (ngrp, 1)), 0.0)
        ti = jnp.where(own, jnp.tile(xi, (ngrp, 1)), 0.0)
        return jnp.concatenate([tr, ti], axis=1)

    pw = [(jnp.ones_like(lam_r), jnp.zeros_like(lam_r))]
    for _ in range(lc):
        pr, pi = pw[-1]
        pw.append((pr * lam_r - pi * lam_i, pr * lam_i + pi * lam_r))
    adec_ref[0:1, :] = pw[lc][0]
    adec_ref[1:2, :] = pw[lc][1]

    for s in range(lc):
        pr, pi = pw[lc - 1 - s]
        wcs_ref[s * rows:(s + 1) * rows, :] = slab(pr * bb_r - pi * bb_i,
                                                   pr * bb_i + pi * bb_r).astype(bf16)

    wq = []
    for t in range(lc + 1):
        pr, pi = pw[t]
        wq.append(slab(cr * pr - ci * pi, -(cr * pi + ci * pr)))
    for r in range(lc):
        wout_ref[r * rows:(r + 1) * rows, :] = wq[r + 1].astype(bf16)

    bsl = _split3(slab(bb_r, bb_i))
    kt = []
    for t in range(lc):
        q = _split3(wq[t])
        acc = _dot_nt(bsl[0], q[0])
        acc += _dot_nt(bsl[0], q[1]) + _dot_nt(bsl[1], q[0])
        acc += _dot_nt(bsl[1], q[1]) + _dot_nt(bsl[0], q[2]) + _dot_nt(bsl[2], q[0])
        kt.append(acc)
    zero = jnp.zeros((rows, rows), f32)
    for s in range(lc):
        for r in range(lc):
            blk = kt[r - s] if r >= s else zero
            wtz_ref[s * rows:(s + 1) * rows, r * rows:(r + 1) * rows] = blk.astype(bf16)


def _s5_weights(lam_re, lam_im, log_dt, b_re, b_im, c_re, c_im, lc):
    G, P = lam_re.shape
    Cg = b_re.shape[-1]
    gb = LANES // Cg
    nb = G // gb
    nl = gb * P
    rows = gb * Cg
    lre = lam_re.reshape(nb, 1, nl)
    lim = lam_im.reshape(nb, 1, nl)
    ldt = jnp.repeat(log_dt, P).reshape(nb, 1, nl)
    brt = b_re.reshape(nb, gb, P, Cg).transpose(0, 3, 1, 2).reshape(nb, Cg, nl)
    bit = b_im.reshape(nb, gb, P, Cg).transpose(0, 3, 1, 2).reshape(nb, Cg, nl)
    crt = c_re.reshape(nb, gb, Cg, P).transpose(0, 2, 1, 3).reshape(nb, Cg, nl)
    cit = c_im.reshape(nb, gb, Cg, P).transpose(0, 2, 1, 3).reshape(nb, Cg, nl)
    vec = pl.BlockSpec((None, 1, nl), lambda k: (k, 0, 0))
    mat = pl.BlockSpec((None, Cg, nl), lambda k: (k, 0, 0))
    return pl.pallas_call(
        functools.partial(_s5_weights_kernel, lc=lc, cg=Cg, p_state=P),
        grid=(nb,),
        in_specs=[vec, vec, vec, mat, mat, mat, mat],
        out_specs=[
            pl.BlockSpec((None, lc * rows, 2 * nl), lambda k: (k, 0, 0)),
            pl.BlockSpec((None, lc * rows, 2 * nl), lambda k: (k, 0, 0)),
            pl.BlockSpec((None, lc * rows, lc * rows), lambda k: (k, 0, 0)),
            pl.BlockSpec((None, 2, nl), lambda k: (k, 0, 0)),
        ],
        out_shape=[
            jax.ShapeDtypeStruct((nb, lc * rows, 2 * nl), bf16),
            jax.ShapeDtypeStruct((nb, lc * rows, 2 * nl), bf16),
            jax.ShapeDtypeStruct((nb, lc * rows, lc * rows), bf16),
            jax.ShapeDtypeStruct((nb, 2, nl), f32),
        ],
        compiler_params=_cparams("parallel"),
        name="s5_weights",
    )(lre, lim, ldt, brt, bit, crt, cit)


def _s5_core_kernel(u_ref, wcs_ref, wout_ref, wtz_ref, adec_ref, d_ref, z_ref,
                    st_ref, y1_ref, y_ref, *, lc, nseg, seg_len):
    nslab = st_ref.shape[0]
    half = nslab // 2
    seg_tok = seg_len * lc

    def phase_a(seg, carry):
        tok0 = seg * seg_tok
        x = jnp.concatenate(
            [u_ref[pl.ds(tok0 + s, seg_len, stride=lc), :] for s in range(lc)], axis=1)
        xb = x.astype(bf16)
        e = _dot(xb, wcs_ref[...])
        for j in range(nslab):
            st_ref[j, pl.ds(seg, seg_len, stride=nseg), :] = e[:, j * LANES:(j + 1) * LANES]
        y1_ref[pl.ds(pl.multiple_of(seg * seg_len, seg_len), seg_len), :] = _dot(xb, wtz_ref[...])
        return carry

    lax.fori_loop(0, nseg, phase_a, 0)

    a_r = [jnp.broadcast_to(adec_ref[0:1, j * LANES:(j + 1) * LANES], (nseg, LANES))
           for j in range(half)]
    a_i = [jnp.broadcast_to(adec_ref[1:2, j * LANES:(j + 1) * LANES], (nseg, LANES))
           for j in range(half)]

    def step(i, hs, store):
        out = []
        rows = pl.ds(pl.multiple_of(i * nseg, nseg), nseg)
        for j in range(half):
            hr, hi = hs[2 * j], hs[2 * j + 1]
            er = st_ref[j, rows, :]
            ei = st_ref[half + j, rows, :]
            if store:
                st_ref[j, rows, :] = hr
                st_ref[half + j, rows, :] = hi
            out.append(a_r[j] * hr - a_i[j] * hi + er)
            out.append(a_r[j] * hi + a_i[j] * hr + ei)
        return tuple(out)

    zeros = tuple(jnp.zeros((nseg, LANES), f32) for _ in range(nslab))
    ends = lax.fori_loop(0, seg_len, lambda i, hs: step(i, hs, False), zeros)

    starts = []
    for j in range(half):
        pr, pi = a_r[j][0:1, :], a_i[j][0:1, :]
        n = seg_len
        while n > 1:
            pr, pi = pr * pr - pi * pi, 2.0 * pr * pi
            n //= 2
        er, ei = ends[2 * j], ends[2 * j + 1]
        sr = [jnp.zeros((1, LANES), f32)]
        si = [jnp.zeros((1, LANES), f32)]
        for k in range(1, nseg):
            sr.append(pr * sr[-1] - pi * si[-1] + er[k - 1:k, :])
            si.append(pr * si[k - 1] + pi * sr[k - 1] + ei[k - 1:k, :])
        starts.append(jnp.concatenate(sr, axis=0))
        starts.append(jnp.concatenate(si, axis=0))
    lax.fori_loop(0, seg_len, lambda i, hs: step(i, hs, True), tuple(starts))

    def phase_c(seg, carry):
        h = jnp.concatenate([st_ref[j, pl.ds(seg, seg_len, stride=nseg), :]
                             for j in range(nslab)], axis=1)
        y = _dot_nt(h.astype(bf16), wout_ref[...])
        y = y + y1_ref[pl.ds(pl.multiple_of(seg * seg_len, seg_len), seg_len), :]
        tok0 = seg * seg_tok
        for r in range(lc):
            y_ref[pl.ds(tok0 + r, seg_len, stride=lc), :] = y[:, r * LANES:(r + 1) * LANES]
        return carry

    lax.fori_loop(0, nseg, phase_c, 0)

    d = d_ref[...]

    def phase_d(t, carry):
        rows = pl.ds(pl.multiple_of(t * seg_tok, seg_tok), seg_tok)
        z_ref[rows, :] = _gelu_tanh(y_ref[rows, :] + d * u_ref[rows, :]).astype(z_ref.dtype)
        return carry

    lax.fori_loop(0, nseg, phase_d, 0)


def _s5_core(u, wcs, wout, wtz, adec, d_skip, lc):
    B, S, D = u.shape
    nb = D // LANES
    nseg = S5_SEGMENTS
    m = S // lc
    seg_len = m // nseg
    assert seg_len * nseg * lc == S and seg_len & (seg_len - 1) == 0
    nslab = wcs.shape[-1] // LANES
    kw = wcs.shape[1]
    kern = functools.partial(_s5_core_kernel, lc=lc, nseg=nseg, seg_len=seg_len)
    return pl.pallas_call(
        kern,
        grid=(nb, B),
        in_specs=[
            pl.BlockSpec((None, S, LANES), lambda k, b: (b, 0, k)),
            pl.BlockSpec((None, kw, nslab * LANES), lambda k, b: (k, 0, 0)),
            pl.BlockSpec((None, kw, nslab * LANES), lambda k, b: (k, 0, 0)),
            pl.BlockSpec((None, kw, kw), lambda k, b: (k, 0, 0)),
            pl.BlockSpec((None, 2, adec.shape[-1]), lambda k, b: (k, 0, 0)),
            pl.BlockSpec((1, LANES), lambda k, b: (0, k)),
        ],
        out_specs=pl.BlockSpec((None, S, LANES), lambda k, b: (b, 0, k)),
        out_shape=jax.ShapeDtypeStruct((B, S, D), bf16),
        scratch_shapes=[
            pltpu.VMEM((nslab, nseg * seg_len, LANES), f32),
            pltpu.VMEM((m, kw), f32),
            pltpu.VMEM((S, LANES), f32),
        ],
        compiler_params=_cparams("parallel", "parallel"),
        name="s5_core",
    )(u, wcs, wout, wtz, adec, d_skip)


def _s5_out_kernel(z_ref, x_ref, mod_ref, wg_ref, wo_ref, h_ref):
    z = z_ref[...]
    g = _dot(z, wg_ref[...])
    z2 = z.astype(f32) * (1.0 / (1.0 + jnp.exp(-g)))
    m = _dot(z2.astype(bf16), wo_ref[...])
    h_ref[...] = x_ref[...] + mod_ref[2:3, :] * m


def _s5_out(z, x, mod4, l, wg_bf, wo_bf, tm):
    B, S, D = x.shape
    row = pl.BlockSpec((None, tm, D), lambda b, i: (b, i, 0))
    return pl.pallas_call(
        _s5_out_kernel,
        grid=(B, S // tm),
        in_specs=[row, row, _mod_spec(D)(l), _resident((D, D)), _resident((D, D))],
        out_specs=row,
        out_shape=jax.ShapeDtypeStruct((B, S, D), f32),
        compiler_params=_cparams("parallel", "parallel"),
        name="s5_out",
    )(z, x, mod4, wg_bf, wo_bf)


def _ffn_kernel(h_ref, halo_ref, mod_ref, g_ref, wa_ref, wb_ref, wd_ref, cw_ref, cb_ref,
                fg_ref, o_ref, hn_ref, a_ref, acc_ref, *, taps, final_norm):
    i = pl.program_id(1)
    tm = h_ref.shape[0]
    hb = halo_ref.shape[0]
    g, shift, scale = g_ref[...], mod_ref[0:1, :], mod_ref[1:2, :]
    x = h_ref[...]
    keep = (i > 0).astype(f32)
    hn_ref[0:hb, :] = (_modulate(halo_ref[...], g, shift, scale) * keep).astype(bf16)
    hn_ref[hb:, :] = _modulate(x, g, shift, scale).astype(bf16)
    acc_ref[...] = jnp.zeros_like(acc_ref)

    def chunk(j, carry):
        a_ref[...] = _dot(hn_ref[...], wa_ref[j])
        b = _dot(hn_ref[hb:, :], wb_ref[j])
        cw = cw_ref[j]
        ac = cb_ref[j]
        for t in range(taps):
            off = hb - (taps - 1) + t
            ac = ac + cw[t:t + 1, :] * a_ref[off:off + tm, :]
        gte = (_silu(ac) * b).astype(bf16)
        acc_ref[...] += _dot(gte, wd_ref[j])
        return carry

    lax.fori_loop(0, wa_ref.shape[0], chunk, 0)
    out = x + mod_ref[2:3, :] * acc_ref[...]
    if final_norm:
        ms = jnp.mean(out * out, axis=-1, keepdims=True)
        out = out * lax.rsqrt(ms + EPS) * fg_ref[...]
    o_ref[...] = out


def _ffn(h, mod4, l, g, w_up, conv_w, conv_b, w_down, final_g, tm, fc, final_norm):
    B, S, D = h.shape
    F = w_down.shape[0]
    taps = conv_w.shape[0]
    nc = F // fc
    hb = BF16_ROWS
    wa = w_up[:, :F].reshape(D, nc, fc).transpose(1, 0, 2).astype(bf16)
    wb = w_up[:, F:].reshape(D, nc, fc).transpose(1, 0, 2).astype(bf16)
    wd = w_down.reshape(nc, fc, D).astype(bf16)
    cw = conv_w.reshape(taps, nc, fc).transpose(1, 0, 2)
    cb = conv_b.reshape(nc, 1, fc)
    per = tm // hb
    kern = functools.partial(_ffn_kernel, taps=taps, final_norm=final_norm)
    return pl.pallas_call(
        kern,
        grid=(B, S // tm),
        in_specs=[
            pl.BlockSpec((None, tm, D), lambda b, i: (b, i, 0)),
            pl.BlockSpec((None, hb, D), lambda b, i: (b, jnp.maximum(i * per - 1, 0), 0)),
            _mod_spec(D)(l),
            _resident((1, D)),
            _resident((nc, D, fc)),
            _resident((nc, D, fc)),
            _resident((nc, fc, D)),
            _resident((nc, taps, fc)),
            _resident((nc, 1, fc)),
            _resident((1, D)),
        ],
        out_specs=pl.BlockSpec((None, tm, D), lambda b, i: (b, i, 0)),
        out_shape=jax.ShapeDtypeStruct((B, S, D), f32),
        scratch_shapes=[
            pltpu.VMEM((hb + tm, D), bf16),
            pltpu.VMEM((hb + tm, fc), f32),
            pltpu.VMEM((tm, D), f32),
        ],
        compiler_params=_cparams("parallel", "parallel"),
        name="ffn",
    )(h, h, mod4, g, wa, wb, wd, cw, cb, final_g)


def _fox_in_kernel(h_ref, mod_ref, g_ref, wqkv_ref, wf_ref, bf_ref, qkv_ref, fc_ref,
                   carry_ref, *, q_scale):
    i = pl.program_id(1)
    tm, D = h_ref.shape

    @pl.when(i == 0)
    def _():
        carry_ref[...] = jnp.zeros_like(carry_ref)

    hn = _modulate(h_ref[...], g_ref[...], mod_ref[0:1, :], mod_ref[1:2, :]).astype(bf16)
    qkv = _dot(hn, wqkv_ref[...])
    qkv_ref[:, :D] = (qkv[:, :D] * q_scale).astype(bf16)
    qkv_ref[:, D:] = qkv[:, D:].astype(bf16)

    fl = _dot(hn, wf_ref[...]) + bf_ref[...]
    lf = jnp.minimum(fl, 0.0) - jnp.log(1.0 + jnp.exp(-jnp.abs(fl)))
    r = lax.broadcasted_iota(jnp.int32, (tm, tm), 0)
    c = lax.broadcasted_iota(jnp.int32, (tm, tm), 1)
    tri = (c <= r).astype(bf16)
    p0, p1, p2 = _split3(lf)
    cs = _dot(tri, p0) + _dot(tri, p1) + _dot(tri, p2) + carry_ref[...]
    fc_ref[...] = cs
    carry_ref[...] = cs[tm - 1:tm, :]


def _fox_in(h, mod4, l, g, wqkv_bf, wf_bf, bf_pad, q_scale, tm):
    B, S, D = h.shape
    return pl.pallas_call(
        functools.partial(_fox_in_kernel, q_scale=q_scale),
        grid=(B, S // tm),
        in_specs=[
            pl.BlockSpec((None, tm, D), lambda b, i: (b, i, 0)),
            _mod_spec(D)(l),
            _resident((1, D)),
            _resident((D, 3 * D)),
            _resident((D, LANES)),
            _resident((1, LANES)),
        ],
        out_specs=[
            pl.BlockSpec((None, tm, 3 * D), lambda b, i: (b, i, 0)),
            pl.BlockSpec((None, tm, LANES), lambda b, i: (b, i, 0)),
        ],
        out_shape=[
            jax.ShapeDtypeStruct((B, S, 3 * D), bf16),
            jax.ShapeDtypeStruct((B, S, LANES), f32),
        ],
        scratch_shapes=[pltpu.VMEM((1, LANES), f32)],
        compiler_params=_cparams("parallel", "arbitrary"),
        name="fox_in",
    )(h, mod4, g, wqkv_bf, wf_bf, bf_pad)


NEG_BIG = -1e30


def _attn_kernel(q_ref, k_ref, v_ref, fq_ref, fk_ref, o_ref, *, tk, dh):
    qi = pl.program_id(2)
    tq = q_ref.shape[0]
    q = q_ref[...]
    lane = lax.broadcasted_iota(jnp.int32, (tq, LANES), 1)
    heads = LANES // dh
    qs = [jnp.where((lane // dh) == e, q, jnp.zeros_like(q)) for e in range(heads)]
    fqs = [fq_ref[e] for e in range(heads)]

    def block(j, carry, diag):
        ks = pl.ds(pl.multiple_of(j * tk, tk), tk)
        kb = k_ref[ks, :]
        vb = v_ref[ks, :]
        out = []
        for e in range(heads):
            m, l, acc = carry[3 * e:3 * e + 3]
            s = _dot_nt(qs[e], kb) + fqs[e] - fk_ref[e, j]
            if diag:
                row = lax.broadcasted_iota(jnp.int32, (tq, tk), 0)
                col = lax.broadcasted_iota(jnp.int32, (tq, tk), 1)
                s = jnp.where(col <= row, s, NEG_BIG)
            m_new = jnp.maximum(m, jnp.max(s, axis=-1, keepdims=True))
            p = jnp.exp(s - m_new)
            alpha = jnp.exp(m - m_new)
            l = alpha * l + jnp.sum(p, axis=-1, keepdims=True)
            acc = alpha * acc + _dot(p.astype(bf16), vb)
            out += [m_new, l, acc]
        return tuple(out)

    init = []
    for e in range(heads):
        init += [jnp.full((tq, 1), NEG_BIG, f32), jnp.zeros((tq, 1), f32),
                 jnp.zeros((tq, LANES), f32)]
    carry = lax.fori_loop(0, qi, lambda j, c: block(j, c, False), tuple(init))
    carry = block(qi, carry, True)
    o = jnp.zeros((tq, LANES), f32)
    for e in range(heads):
        o = jnp.where((lane // dh) == e, carry[3 * e + 2] / carry[3 * e + 1], o)
    o_ref[...] = o.astype(o_ref.dtype)


def _attn(qkv, fcol, frow, n_heads, tq):
    B, S, D3 = qkv.shape
    D = D3 // 3
    dh = D // n_heads
    heads = LANES // dh
    nhp = D // LANES
    nk = S // tq
    kern = functools.partial(_attn_kernel, tk=tq, dh=dh)
    return pl.pallas_call(
        kern,
        grid=(B, nhp, S // tq),
        in_specs=[
            pl.BlockSpec((None, tq, LANES), lambda b, h, i: (b, i, h)),
            pl.BlockSpec((None, S, LANES), lambda b, h, i: (b, 0, nhp + h)),
            pl.BlockSpec((None, S, LANES), lambda b, h, i: (b, 0, 2 * nhp + h)),
            pl.BlockSpec((None, heads, tq, 1), lambda b, h, i: (b, h, i, 0)),
            pl.BlockSpec((None, heads, nk, 1, tq), lambda b, h, i: (b, h, 0, 0, 0)),
        ],
        out_specs=pl.BlockSpec((None, tq, LANES), lambda b, h, i: (b, i, h)),
        out_shape=jax.ShapeDtypeStruct((B, S, D), bf16),
        compiler_params=_cparams("parallel", "parallel", "arbitrary"),
        name="attn",
    )(qkv, qkv, qkv, fcol, frow)


def _fox_out_kernel(o_ref, h_ref, mod_ref, w_ref, out_ref):
    out_ref[...] = h_ref[...] + mod_ref[2:3, :] * _dot(o_ref[...], w_ref[...])


def _fox_out(o, h, mod4, l, w_bf, tm):
    B, S, D = h.shape
    row = pl.BlockSpec((None, tm, D), lambda b, i: (b, i, 0))
    return pl.pallas_call(
        _fox_out_kernel,
        grid=(B, S // tm),
        in_specs=[row, row, _mod_spec(D)(l), _resident((D, D))],
        out_specs=row,
        out_shape=jax.ShapeDtypeStruct((B, S, D), f32),
        compiler_params=_cparams("parallel", "parallel"),
        name="fox_out",
    )(o, h, mod4, w_bf)


def kernel(x, c, norm_g, ada_w, ada_b, s5_w_in, s5_lam_re, s5_lam_im, s5_log_dt, s5_b_re, s5_b_im, s5_c_re, s5_c_im, s5_d, s5_w_glu, s5_w_out, fox_w_in, fox_b_f, fox_w_out, ffn_w_up, ffn_conv_w, ffn_conv_b, ffn_w_down, final_g):
    B, S, D = x.shape
    depth = norm_g.shape[0]
    n_heads = fox_b_f.shape[-1]
    tm = min(512, S)
    fc = 256
    final = final_g.reshape(1, D)

    mod = _adaln(c, ada_w, ada_b)
    mod4 = mod.reshape(depth * 2, B, 3, D)

    h = x
    for i in range(depth):
        j = i // 2
        g_mix = norm_g[i, 0].reshape(1, D)
        g_ffn = norm_g[i, 1].reshape(1, D)
        if i % 2 == 0:
            u = _s5_in(h, mod4, 2 * i, g_mix, s5_w_in[j].astype(bf16), tm)
            wcs, wout, wtz, adec = _s5_weights(s5_lam_re[j], s5_lam_im[j], s5_log_dt[j],
                                               s5_b_re[j], s5_b_im[j], s5_c_re[j], s5_c_im[j],
                                               S5_CHUNK)
            z = _s5_core(u, wcs, wout, wtz, adec, s5_d[j].reshape(1, D), S5_CHUNK)
            h = _s5_out(z, h, mod4, 2 * i, s5_w_glu[j].astype(bf16),
                        s5_w_out[j].astype(bf16), tm)
        else:
            w_in = fox_w_in[j]
            wqkv = w_in[:, :3 * D].astype(bf16)
            wf = jnp.pad(w_in[:, 3 * D:], ((0, 0), (0, LANES - n_heads))).astype(bf16)
            bfp = jnp.pad(fox_b_f[j], (0, LANES - n_heads)).reshape(1, LANES)
            qkv, fcum = _fox_in(h, mod4, 2 * i, g_mix, wqkv, wf, bfp,
                                float((D // n_heads) ** -0.5), tm)
            ft = fcum[:, :, :n_heads].transpose(0, 2, 1)
            tq = min(512, S)
            fcol = ft.reshape(B, n_heads, S, 1)
            frow = ft.reshape(B, n_heads, S // tq, 1, tq)
            o = _attn(qkv, fcol, frow, n_heads, tq)
            h = _fox_out(o, h, mod4, 2 * i, fox_w_out[j].astype(bf16), tm)
        h = _ffn(h, mod4, 2 * i + 1, g_ffn, ffn_w_up[i], ffn_conv_w[i], ffn_conv_b[i],
                 ffn_w_down[i], final, tm, fc, final_norm=(i == depth - 1))
    return h
```

```python
import functools

import jax
import jax.numpy as jnp
from jax import lax
from jax.experimental import pallas as pl
from jax.experimental.pallas import tpu as pltpu

EPS = 1e-6
LANES = 128
SUBLANES = 8
BF16_ROWS = 16
VMEM_LIMIT = 56 * 1024 * 1024

S5_CHUNK = 4
S5_SEGMENTS = SUBLANES

f32 = jnp.float32
bf16 = jnp.bfloat16


def _cparams(*sem):
    return pltpu.CompilerParams(dimension_semantics=sem, vmem_limit_bytes=VMEM_LIMIT)


def _resident(shape):
    nd = len(shape)
    return pl.BlockSpec(shape, lambda *_: (0,) * nd, pipeline_mode=pl.Buffered(1))


def _dot(a, b):
    return jnp.dot(a, b, preferred_element_type=f32)


def _dot_nt(a, b):
    return lax.dot_general(a, b, (((1,), (1,)), ((), ())), preferred_element_type=f32)


def _split3(x):
    hi = x.astype(bf16)
    r1 = x - hi.astype(f32)
    mid = r1.astype(bf16)
    lo = (r1 - mid.astype(f32)).astype(bf16)
    return hi, mid, lo


def _modulate(x, g, shift, scale):
    ms = jnp.mean(x * x, axis=-1, keepdims=True)
    return x * lax.rsqrt(ms + EPS) * g * (1.0 + scale) + shift


def _silu(x):
    return x * (1.0 / (1.0 + jnp.exp(-x)))


def _gelu_tanh(x):
    c = 0.7978845608028654
    return 0.5 * x * (1.0 + jnp.tanh(c * (x + 0.044715 * (x * x * x))))


def _adaln_kernel(ct_ref, w_ref, b_ref, o_ref):
    ct = ct_ref[...]
    cs = _silu(ct)
    w = w_ref[...]
    for b in range(ct.shape[1]):
        col = cs[:, b:b + 1]
        o_ref[b:b + 1, :] = jnp.sum(w * col, axis=0, keepdims=True) + b_ref[...]


def _adaln(c, ada_w, ada_b):
    B, D = c.shape
    L = ada_w.shape[0] * ada_w.shape[1]
    N = ada_w.shape[-1]
    tn = 512
    w = ada_w.reshape(L, D, N)
    bias = ada_b.reshape(L, 1, N)
    return pl.pallas_call(
        _adaln_kernel,
        grid=(L, N // tn),
        in_specs=[
            pl.BlockSpec((D, B), lambda l, j: (0, 0)),
            pl.BlockSpec((None, D, tn), lambda l, j: (l, 0, j)),
            pl.BlockSpec((None, 1, tn), lambda l, j: (l, 0, j)),
        ],
        out_specs=pl.BlockSpec((None, B, tn), lambda l, j: (l, 0, j)),
        out_shape=jax.ShapeDtypeStruct((L, B, N), f32),
        compiler_params=_cparams("parallel", "parallel"),
        name="adaln",
    )(c.T, w, bias)


def _mod_spec(D):
    return lambda l: pl.BlockSpec((None, None, 3, D), lambda b, i: (l, b, 0, 0))


def _s5_in_kernel(x_ref, mod_ref, g_ref, w_ref, u_ref):
    x = x_ref[...]
    hn = _modulate(x, g_ref[...], mod_ref[0:1, :], mod_ref[1:2, :])
    u_ref[...] = _dot(hn.astype(bf16), w_ref[...])


def _s5_in(x, mod4, l, g, w_bf, tm):
    B, S, D = x.shape
    return pl.pallas_call(
        _s5_in_kernel,
        grid=(B, S // tm),
        in_specs=[
            pl.BlockSpec((None, tm, D), lambda b, i: (b, i, 0)),
            _mod_spec(D)(l),
            _resident((1, D)),
            _resident((D, D)),
        ],
        out_specs=pl.BlockSpec((None, tm, D), lambda b, i: (b, i, 0)),
        out_shape=jax.ShapeDtypeStruct((B, S, D), f32),
        compiler_params=_cparams("parallel", "parallel"),
        name="s5_in",
    )(x, mod4, g, w_bf)


def _s5_weights_kernel(lre_ref, lim_ref, ldt_ref, br_ref, bi_ref, cr_ref, ci_ref,
                       wcs_ref, wout_ref, wtz_ref, adec_ref, *, lc, cg, p_state):
    lre = lre_ref[...]
    lim = lim_ref[...]
    dt = jnp.exp(ldt_ref[...])
    nl = lre.shape[1]
    mag = jnp.exp(lre * dt)
    lam_r = mag * jnp.cos(lim * dt)
    lam_i = mag * jnp.sin(lim * dt)
    num_r = lam_r - 1.0
    den = lre * lre + lim * lim
    k_r = (num_r * lre + lam_i * lim) / den
    k_i = (lam_i * lre - num_r * lim) / den
    br = br_ref[...]
    bi = bi_ref[...]
    bb_r = k_r * br - k_i * bi
    bb_i = k_r * bi + k_i * br
    cr = cr_ref[...]
    ci = ci_ref[...]

    ngrp = nl // p_state
    rows = ngrp * cg
    row_grp = lax.broadcasted_iota(jnp.int32, (rows, nl), 0) // cg
    lane_grp = lax.broadcasted_iota(jnp.int32, (rows, nl), 1) // p_state
    own = row_grp == lane_grp

    def slab(xr, xi):
        tr = jnp.where(own, jnp.tile(xr, (ngrp, 1)), 0.0)
        ti = jnp.where(own, jnp.tile(xi, (ngrp, 1)), 0.0)
        return jnp.concatenate([tr, ti], axis=1)

    pw = [(jnp.ones_like(lam_r), jnp.zeros_like(lam_r))]
    for _ in range(lc):
        pr, pi = pw[-1]
        pw.append((pr * lam_r - pi * lam_i, pr * lam_i + pi * lam_r))
    adec_ref[0:1, :] = pw[lc][0]
    adec_ref[1:2, :] = pw[lc][1]

    for s in range(lc):
        pr, pi = pw[lc - 1 - s]
        wcs_ref[s * rows:(s + 1) * rows, :] = slab(pr * bb_r - pi * bb_i,
                                                   pr * bb_i + pi * bb_r).astype(bf16)

    wq = []
    for t in range(lc + 1):
        pr, pi = pw[t]
        wq.append(slab(cr * pr - ci * pi, -(cr * pi + ci * pr)))
    for r in range(lc):
        wout_ref[r * rows:(r + 1) * rows, :] = wq[r + 1].astype(bf16)

    bsl = _split3(slab(bb_r, bb_i))
    kt = []
    for t in range(lc):
        q = _split3(wq[t])
        acc = _dot_nt(bsl[0], q[0])
        acc += _dot_nt(bsl[0], q[1]) + _dot_nt(bsl[1], q[0])
        acc += _dot_nt(bsl[1], q[1]) + _dot_nt(bsl[0], q[2]) + _dot_nt(bsl[2], q[0])
        kt.append(acc)
    zero = jnp.zeros((rows, rows), f32)
    for s in range(lc):
        for r in range(lc):
            blk = kt[r - s] if r >= s else zero
            wtz_ref[s * rows:(s + 1) * rows, r * rows:(r + 1) * rows] = blk.astype(bf16)


def _s5_weights(lam_re, lam_im, log_dt, b_re, b_im, c_re, c_im, lc):
    G, P = lam_re.shape
    Cg = b_re.shape[-1]
    gb = LANES // Cg
    nb = G // gb
    nl = gb * P
    rows = gb * Cg
    lre = lam_re.reshape(nb, 1, nl)
    lim = lam_im.reshape(nb, 1, nl)
    ldt = jnp.repeat(log_dt, P).reshape(nb, 1, nl)
    brt = b_re.reshape(nb, gb, P, Cg).transpose(0, 3, 1, 2).reshape(nb, Cg, nl)
    bit = b_im.reshape(nb, gb, P, Cg).transpose(0, 3, 1, 2).reshape(nb, Cg, nl)
    crt = c_re.reshape(nb, gb, Cg, P).transpose(0, 2, 1, 3).reshape(nb, Cg, nl)
    cit = c_im.reshape(nb, gb, Cg, P).transpose(0, 2, 1, 3).reshape(nb, Cg, nl)
    vec = pl.BlockSpec((None, 1, nl), lambda k: (k, 0, 0))
    mat = pl.BlockSpec((None, Cg, nl), lambda k: (k, 0, 0))
    return pl.pallas_call(
        functools.partial(_s5_weights_kernel, lc=lc, cg=Cg, p_state=P),
        grid=(nb,),
        in_specs=[vec, vec, vec, mat, mat, mat, mat],
        out_specs=[
            pl.BlockSpec((None, lc * rows, 2 * nl), lambda k: (k, 0, 0)),
            pl.BlockSpec((None, lc * rows, 2 * nl), lambda k: (k, 0, 0)),
            pl.BlockSpec((None, lc * rows, lc * rows), lambda k: (k, 0, 0)),
            pl.BlockSpec((None, 2, nl), lambda k: (k, 0, 0)),
        ],
        out_shape=[
            jax.ShapeDtypeStruct((nb, lc * rows, 2 * nl), bf16),
            jax.ShapeDtypeStruct((nb, lc * rows, 2 * nl), bf16),
            jax.ShapeDtypeStruct((nb, lc * rows, lc * rows), bf16),
            jax.ShapeDtypeStruct((nb, 2, nl), f32),
        ],
        compiler_params=_cparams("parallel"),
        name="s5_weights",
    )(lre, lim, ldt, brt, bit, crt, cit)


def _s5_core_kernel(u_ref, wcs_ref, wout_ref, wtz_ref, adec_ref, d_ref, z_ref,
                    st_ref, y1_ref, y_ref, *, lc, nseg, seg_len):
    nslab = st_ref.shape[0]
    half = nslab // 2
    seg_tok = seg_len * lc

    def phase_a(seg, carry):
        tok0 = seg * seg_tok
        x = jnp.concatenate(
            [u_ref[pl.ds(tok0 + s, seg_len, stride=lc), :] for s in range(lc)], axis=1)
        xb = x.astype(bf16)
        e = _dot(xb, wcs_ref[...])
        for j in range(nslab):
            st_ref[j, pl.ds(seg, seg_len, stride=nseg), :] = e[:, j * LANES:(j + 1) * LANES]
        y1_ref[pl.ds(pl.multiple_of(seg * seg_len, seg_len), seg_len), :] = _dot(xb, wtz_ref[...])
        return carry

    lax.fori_loop(0, nseg, phase_a, 0)

    a_r = [jnp.broadcast_to(adec_ref[0:1, j * LANES:(j + 1) * LANES], (nseg, LANES))
           for j in range(half)]
    a_i = [jnp.broadcast_to(adec_ref[1:2, j * LANES:(j + 1) * LANES], (nseg, LANES))
           for j in range(half)]

    def step(i, hs, store):
        out = []
        rows = pl.ds(pl.multiple_of(i * nseg, nseg), nseg)
        for j in range(half):
            hr, hi = hs[2 * j], hs[2 * j + 1]
            er = st_ref[j, rows, :]
            ei = st_ref[half + j, rows, :]
            if store:
                st_ref[j, rows, :] = hr
                st_ref[half + j, rows, :] = hi
            out.append(a_r[j] * hr - a_i[j] * hi + er)
            out.append(a_r[j] * hi + a_i[j] * hr + ei)
        return tuple(out)

    zeros = tuple(jnp.zeros((nseg, LANES), f32) for _ in range(nslab))
    ends = lax.fori_loop(0, seg_len, lambda i, hs: step(i, hs, False), zeros)

    starts = []
    for j in range(half):
        pr, pi = a_r[j][0:1, :], a_i[j][0:1, :]
        n = seg_len
        while n > 1:
            pr, pi = pr * pr - pi * pi, 2.0 * pr * pi
            n //= 2
        er, ei = ends[2 * j], ends[2 * j + 1]
        sr = [jnp.zeros((1, LANES), f32)]
        si = [jnp.zeros((1, LANES), f32)]
        for k in range(1, nseg):
            sr.append(pr * sr[-1] - pi * si[-1] + er[k - 1:k, :])
            si.append(pr * si[k - 1] + pi * sr[k - 1] + ei[k - 1:k, :])
        starts.append(jnp.concatenate(sr, axis=0))
        starts.append(jnp.concatenate(si, axis=0))
    lax.fori_loop(0, seg_len, lambda i, hs: step(i, hs, True), tuple(starts))

    def phase_c(seg, carry):
        h = jnp.concatenate([st_ref[j, pl.ds(seg, seg_len, stride=nseg), :]
                             for j in range(nslab)], axis=1)
        y = _dot_nt(h.astype(bf16), wout_ref[...])
        y = y + y1_ref[pl.ds(pl.multiple_of(seg * seg_len, seg_len), seg_len), :]
        tok0 = seg * seg_tok
        for r in range(lc):
            y_ref[pl.ds(tok0 + r, seg_len, stride=lc), :] = y[:, r * LANES:(r + 1) * LANES]
        return carry

    lax.fori_loop(0, nseg, phase_c, 0)

    d = d_ref[...]

    def phase_d(t, carry):
        rows = pl.ds(pl.multiple_of(t * seg_tok, seg_tok), seg_tok)
        z_ref[rows, :] = _gelu_tanh(y_ref[rows, :] + d * u_ref[rows, :]).astype(z_ref.dtype)
        return carry

    lax.fori_loop(0, nseg, phase_d, 0)


def _s5_core(u, wcs, wout, wtz, adec, d_skip, lc):
    B, S, D = u.shape
    nb = D // LANES
    nseg = S5_SEGMENTS
    m = S // lc
    seg_len = m // nseg
    assert seg_len * nseg * lc == S and seg_len & (seg_len - 1) == 0
    nslab = wcs.shape[-1] // LANES
    kw = wcs.shape[1]
    kern = functools.partial(_s5_core_kernel, lc=lc, nseg=nseg, seg_len=seg_len)
    return pl.pallas_call(
        kern,
        grid=(nb, B),
        in_specs=[
            pl.BlockSpec((None, S, LANES), lambda k, b: (b, 0, k)),
            pl.BlockSpec((None, kw, nslab * LANES), lambda k, b: (k, 0, 0)),
            pl.BlockSpec((None, kw, nslab * LANES), lambda k, b: (k, 0, 0)),
            pl.BlockSpec((None, kw, kw), lambda k, b: (k, 0, 0)),
            pl.BlockSpec((None, 2, adec.shape[-1]), lambda k, b: (k, 0, 0)),
            pl.BlockSpec((1, LANES), lambda k, b: (0, k)),
        ],
        out_specs=pl.BlockSpec((None, S, LANES), lambda k, b: (b, 0, k)),
        out_shape=jax.ShapeDtypeStruct((B, S, D), bf16),
        scratch_shapes=[
            pltpu.VMEM((nslab, nseg * seg_len, LANES), f32),
            pltpu.VMEM((m, kw), f32),
            pltpu.VMEM((S, LANES), f32),
        ],
        compiler_params=_cparams("parallel", "parallel"),
        name="s5_core",
    )(u, wcs, wout, wtz, adec, d_skip)


def _s5_out_kernel(z_ref, x_ref, mod_ref, wg_ref, wo_ref, h_ref):
    z = z_ref[...]
    g = _dot(z, wg_ref[...])
    z2 = z.astype(f32) * (1.0 / (1.0 + jnp.exp(-g)))
    m = _dot(z2.astype(bf16), wo_ref[...])
    h_ref[...] = x_ref[...] + mod_ref[2:3, :] * m


def _s5_out(z, x, mod4, l, wg_bf, wo_bf, tm):
    B, S, D = x.shape
    row = pl.BlockSpec((None, tm, D), lambda b, i: (b, i, 0))
    return pl.pallas_call(
        _s5_out_kernel,
        grid=(B, S // tm),
        in_specs=[row, row, _mod_spec(D)(l), _resident((D, D)), _resident((D, D))],
        out_specs=row,
        out_shape=jax.ShapeDtypeStruct((B, S, D), f32),
        compiler_params=_cparams("parallel", "parallel"),
        name="s5_out",
    )(z, x, mod4, wg_bf, wo_bf)


def _ffn_kernel(h_ref, halo_ref, mod_ref, g_ref, wa_ref, wb_ref, wd_ref, cw_ref, cb_ref,
                fg_ref, o_ref, hn_ref, a_ref, acc_ref, *, taps, final_norm):
    i = pl.program_id(1)
    tm = h_ref.shape[0]
    hb = halo_ref.shape[0]
    g, shift, scale = g_ref[...], mod_ref[0:1, :], mod_ref[1:2, :]
    x = h_ref[...]
    keep = (i > 0).astype(f32)
    hn_ref[0:hb, :] = (_modulate(halo_ref[...], g, shift, scale) * keep).astype(bf16)
    hn_ref[hb:, :] = _modulate(x, g, shift, scale).astype(bf16)
    acc_ref[...] = jnp.zeros_like(acc_ref)

    def chunk(j, carry):
        a_ref[...] = _dot(hn_ref[...], wa_ref[j])
        b = _dot(hn_ref[hb:, :], wb_ref[j])
        cw = cw_ref[j]
        ac = cb_ref[j]
        for t in range(taps):
            off = hb - (taps - 1) + t
            ac = ac + cw[t:t + 1, :] * a_ref[off:off + tm, :]
        gte = (_silu(ac) * b).astype(bf16)
        acc_ref[...] += _dot(gte, wd_ref[j])
        return carry

    lax.fori_loop(0, wa_ref.shape[0], chunk, 0)
    out = x + mod_ref[2:3, :] * acc_ref[...]
    if final_norm:
        ms = jnp.mean(out * out, axis=-1, keepdims=True)
        out = out * lax.rsqrt(ms + EPS) * fg_ref[...]
    o_ref[...] = out


def _ffn(h, mod4, l, g, w_up, conv_w, conv_b, w_down, final_g, tm, fc, final_norm):
    B, S, D = h.shape
    F = w_down.shape[0]
    taps = conv_w.shape[0]
    nc = F // fc
    hb = BF16_ROWS
    wa = w_up[:, :F].reshape(D, nc, fc).transpose(1, 0, 2).astype(bf16)
    wb = w_up[:, F:].reshape(D, nc, fc).transpose(1, 0, 2).astype(bf16)
    wd = w_down.reshape(nc, fc, D).astype(bf16)
    cw = conv_w.reshape(taps, nc, fc).transpose(1, 0, 2)
    cb = conv_b.reshape(nc, 1, fc)
    per = tm // hb
    kern = functools.partial(_ffn_kernel, taps=taps, final_norm=final_norm)
    return pl.pallas_call(
        kern,
        grid=(B, S // tm),
        in_specs=[
            pl.BlockSpec((None, tm, D), lambda b, i: (b, i, 0)),
            pl.BlockSpec((None, hb, D), lambda b, i: (b, jnp.maximum(i * per - 1, 0), 0)),
            _mod_spec(D)(l),
            _resident((1, D)),
            _resident((nc, D, fc)),
            _resident((nc, D, fc)),
            _resident((nc, fc, D)),
            _resident((nc, taps, fc)),
            _resident((nc, 1, fc)),
            _resident((1, D)),
        ],
        out_specs=pl.BlockSpec((None, tm, D), lambda b, i: (b, i, 0)),
        out_shape=jax.ShapeDtypeStruct((B, S, D), f32),
        scratch_shapes=[
            pltpu.VMEM((hb + tm, D), bf16),
            pltpu.VMEM((hb + tm, fc), f32),
            pltpu.VMEM((tm, D), f32),
        ],
        compiler_params=_cparams("parallel", "parallel"),
        name="ffn",
    )(h, h, mod4, g, wa, wb, wd, cw, cb, final_g)


LOG2E = 1.4426950408889634


def _fox_in_kernel(h_ref, mod_ref, g_ref, wqt_ref, wk_ref, wvt_ref, wf_ref, bf_ref,
                   qt_ref, k_ref, vt_ref, fc_ref, carry_ref, *, q_scale, ts):
    i = pl.program_id(1)
    tm, D = h_ref.shape

    @pl.when(i == 0)
    def _():
        carry_ref[...] = jnp.zeros_like(carry_ref)

    hn = _modulate(h_ref[...], g_ref[...], mod_ref[0:1, :], mod_ref[1:2, :]).astype(bf16)
    k_ref[...] = _dot(hn, wk_ref[...]).astype(bf16)
    qt_ref[...] = (_dot_nt(wqt_ref[...], hn) * q_scale).astype(bf16)
    vt = _dot_nt(wvt_ref[...], hn).astype(bf16)
    for c in range(tm // ts):
        vt_ref[c] = vt[:, c * ts:(c + 1) * ts]

    fl = _dot(hn, wf_ref[...]) + bf_ref[...]
    lf = jnp.minimum(fl, 0.0) - jnp.log(1.0 + jnp.exp(-jnp.abs(fl)))
    r = lax.broadcasted_iota(jnp.int32, (tm, tm), 0)
    c = lax.broadcasted_iota(jnp.int32, (tm, tm), 1)
    tri = (c <= r).astype(bf16)
    p0, p1, p2 = _split3(lf)
    cs = _dot(tri, p0) + _dot(tri, p1) + _dot(tri, p2) + carry_ref[...]
    fc_ref[...] = cs * LOG2E
    carry_ref[...] = cs[tm - 1:tm, :]


def _fox_in(h, mod4, l, g, wqt_bf, wk_bf, wvt_bf, wf_bf, bf_pad, q_scale, tm, ts):
    B, S, D = h.shape
    return pl.pallas_call(
        functools.partial(_fox_in_kernel, q_scale=q_scale, ts=ts),
        grid=(B, S // tm),
        in_specs=[
            pl.BlockSpec((None, tm, D), lambda b, i: (b, i, 0)),
            _mod_spec(D)(l),
            _resident((1, D)),
            _resident((D, D)),
            _resident((D, D)),
            _resident((D, D)),
            _resident((D, LANES)),
            _resident((1, LANES)),
        ],
        out_specs=[
            pl.BlockSpec((None, D, tm), lambda b, i: (b, 0, i)),
            pl.BlockSpec((None, tm, D), lambda b, i: (b, i, 0)),
            pl.BlockSpec((None, tm // ts, D, ts), lambda b, i: (b, i, 0, 0)),
            pl.BlockSpec((None, tm, LANES), lambda b, i: (b, i, 0)),
        ],
        out_shape=[
            jax.ShapeDtypeStruct((B, D, S), bf16),
            jax.ShapeDtypeStruct((B, S, D), bf16),
            jax.ShapeDtypeStruct((B, S // ts, D, ts), bf16),
            jax.ShapeDtypeStruct((B, S, LANES), f32),
        ],
        scratch_shapes=[pltpu.VMEM((1, LANES), f32)],
        compiler_params=_cparams("parallel", "arbitrary"),
        name="fox_in",
    )(h, mod4, g, wqt_bf, wk_bf, wvt_bf, wf_bf, bf_pad)


NEG_BIG = -1e30
ATTN_LOOKAHEAD = 3


def _sublane_allreduce(x, op):
    for sh in (4, 2, 1):
        x = op(x, pltpu.roll(x, sh, axis=0))
    return x


def _tree(xs, op):
    while len(xs) > 1:
        xs = [op(xs[i], xs[i + 1]) if i + 1 < len(xs) else xs[i] for i in range(0, len(xs), 2)]
    return xs[0]


def _attn_kernel(qt_ref, k_ref, vt_ref, fk_ref, o_ref, fkb_ref, *, ts, dh):
    qi = pl.program_id(2)
    tq = qt_ref.shape[1]
    heads = LANES // dh
    spq = tq // ts

    @pl.when(qi == 0)
    def _():
        def fill(a, carry):
            for e in range(heads):
                row = fk_ref[e, pl.ds(a, 1), :]
                rows = pl.ds(pl.multiple_of(a * ts, ts), ts)
                fkb_ref[e, rows, :] = jnp.broadcast_to(row, (LANES, ts)).T
            return carry

        lax.fori_loop(0, k_ref.shape[0] // ts, fill, 0)

    qt = qt_ref[...]
    rowi = lax.broadcasted_iota(jnp.int32, qt.shape, 0)
    qts = [jnp.where((rowi // dh) == e, qt, jnp.zeros_like(qt)) for e in range(heads)]
    nrt = ts // SUBLANES

    def scores(j, key_off):
        ks = pl.ds(pl.multiple_of(j * ts, ts), ts)
        kb = k_ref[ks, :]
        out = []
        for e in range(heads):
            fkb = fkb_ref[e, ks, :]
            s = _dot(kb, qts[e]) - jnp.concatenate([fkb] * (tq // LANES), axis=1)
            if key_off is not None:
                kidx = lax.broadcasted_iota(jnp.int32, (ts, tq), 0) + key_off
                qidx = lax.broadcasted_iota(jnp.int32, (ts, tq), 1)
                s = jnp.where(kidx <= qidx, s, NEG_BIG)
            out.append(s)
        return out

    ones_rows = (lax.broadcasted_iota(jnp.int32, (BF16_ROWS, ts), 0) == 0).astype(bf16)
    acc_rows = dh + SUBLANES

    def absorb(j, ss, carry):
        vtb = vt_ref[j]
        out = []
        for e in range(heads):
            m, acc = carry[2 * e:2 * e + 2]
            s = ss[e]
            tiles = [s[r * SUBLANES:(r + 1) * SUBLANES, :] for r in range(nrt)]
            m_new = jnp.maximum(m, _sublane_allreduce(_tree(tiles, jnp.maximum), jnp.maximum))
            alpha = jnp.exp2(m - m_new)
            p = jnp.concatenate([jnp.exp2(t - m_new) for t in tiles], axis=0).astype(bf16)
            lhs = jnp.concatenate([vtb[e * dh:(e + 1) * dh, :], ones_rows], axis=0)
            pv = _dot(lhs, p)[:acc_rows, :]
            acc = jnp.concatenate([alpha] * (acc_rows // SUBLANES), axis=0) * acc + pv
            out += [m_new, acc]
        return tuple(out)

    def group(j0, carry, masked):
        offs = [d * ts if masked else None for d in range(spq)]
        ss = {}
        for d in range(min(ATTN_LOOKAHEAD, spq)):
            ss[d] = scores(j0 + d, offs[d])
        for d in range(spq):
            nxt = d + ATTN_LOOKAHEAD
            if nxt < spq:
                ss[nxt] = scores(j0 + nxt, offs[nxt])
            carry = absorb(j0 + d, ss.pop(d), carry)
        return carry

    init = []
    for e in range(heads):
        init += [jnp.full((SUBLANES, tq), NEG_BIG, f32), jnp.zeros((acc_rows, tq), f32)]
    carry = lax.fori_loop(0, qi, lambda g, c: group(g * spq, c, False), tuple(init))
    carry = group(qi * spq, carry, True)
    outs = []
    for e in range(heads):
        acc = carry[2 * e + 1]
        l = jnp.broadcast_to(acc[dh:dh + 1, :], (dh, tq))
        outs.append(acc[:dh, :] / l)
    o_ref[...] = jnp.concatenate(outs, axis=0).T.astype(o_ref.dtype)


def _attn(qt, k, vt, fk, n_heads, tq):
    B, D, S = qt.shape
    ts = vt.shape[-1]
    dh = D // n_heads
    heads = LANES // dh
    nhp = D // LANES
    kern = functools.partial(_attn_kernel, ts=ts, dh=dh)
    return pl.pallas_call(
        kern,
        grid=(B, nhp, S // tq),
        in_specs=[
            pl.BlockSpec((None, LANES, tq), lambda b, h, i: (b, h, i)),
            pl.BlockSpec((None, S, LANES), lambda b, h, i: (b, 0, h)),
            pl.BlockSpec((None, S // ts, LANES, ts), lambda b, h, i: (b, 0, h, 0)),
            pl.BlockSpec((None, heads, S // ts, ts), lambda b, h, i: (b, h, 0, 0)),
        ],
        out_specs=pl.BlockSpec((None, tq, LANES), lambda b, h, i: (b, i, h)),
        out_shape=jax.ShapeDtypeStruct((B, S, D), bf16),
        scratch_shapes=[pltpu.VMEM((heads, S, LANES), f32)],
        compiler_params=_cparams("parallel", "parallel", "arbitrary"),
        name="attn",
    )(qt, k, vt, fk)


def _fox_out_kernel(o_ref, h_ref, mod_ref, w_ref, out_ref):
    out_ref[...] = h_ref[...] + mod_ref[2:3, :] * _dot(o_ref[...], w_ref[...])


def _fox_out(o, h, mod4, l, w_bf, tm):
    B, S, D = h.shape
    row = pl.BlockSpec((None, tm, D), lambda b, i: (b, i, 0))
    return pl.pallas_call(
        _fox_out_kernel,
        grid=(B, S // tm),
        in_specs=[row, row, _mod_spec(D)(l), _resident((D, D))],
        out_specs=row,
        out_shape=jax.ShapeDtypeStruct((B, S, D), f32),
        compiler_params=_cparams("parallel", "parallel"),
        name="fox_out",
    )(o, h, mod4, w_bf)


def kernel(x, c, norm_g, ada_w, ada_b, s5_w_in, s5_lam_re, s5_lam_im, s5_log_dt, s5_b_re, s5_b_im, s5_c_re, s5_c_im, s5_d, s5_w_glu, s5_w_out, fox_w_in, fox_b_f, fox_w_out, ffn_w_up, ffn_conv_w, ffn_conv_b, ffn_w_down, final_g):
    B, S, D = x.shape
    depth = norm_g.shape[0]
    n_heads = fox_b_f.shape[-1]
    tm = min(512, S)
    fc = 256
    final = final_g.reshape(1, D)

    mod = _adaln(c, ada_w, ada_b)
    mod4 = mod.reshape(depth * 2, B, 3, D)

    h = x
    for i in range(depth):
        j = i // 2
        g_mix = norm_g[i, 0].reshape(1, D)
        g_ffn = norm_g[i, 1].reshape(1, D)
        if i % 2 == 0:
            u = _s5_in(h, mod4, 2 * i, g_mix, s5_w_in[j].astype(bf16), tm)
            wcs, wout, wtz, adec = _s5_weights(s5_lam_re[j], s5_lam_im[j], s5_log_dt[j],
                                               s5_b_re[j], s5_b_im[j], s5_c_re[j], s5_c_im[j],
                                               S5_CHUNK)
            z = _s5_core(u, wcs, wout, wtz, adec, s5_d[j].reshape(1, D), S5_CHUNK)
            h = _s5_out(z, h, mod4, 2 * i, s5_w_glu[j].astype(bf16),
                        s5_w_out[j].astype(bf16), tm)
        else:
            w_in = fox_w_in[j]
            wqt = w_in[:, :D].T.astype(bf16)
            wk = w_in[:, D:2 * D].astype(bf16)
            wvt = w_in[:, 2 * D:3 * D].T.astype(bf16)
            wf = jnp.pad(w_in[:, 3 * D:], ((0, 0), (0, LANES - n_heads))).astype(bf16)
            bfp = jnp.pad(fox_b_f[j], (0, LANES - n_heads)).reshape(1, LANES)
            ts = LANES
            q_scale = float((D // n_heads) ** -0.5 * LOG2E)
            qt, k, vt, fcum = _fox_in(h, mod4, 2 * i, g_mix, wqt, wk, wvt, wf, bfp,
                                      q_scale, tm, ts)
            fk = fcum[:, :, :n_heads].transpose(0, 2, 1).reshape(B, n_heads, S // ts, ts)
            o = _attn(qt, k, vt, fk, n_heads, tq=min(512, S))
            h = _fox_out(o, h, mod4, 2 * i, fox_w_out[j].astype(bf16), tm)
        h = _ffn(h, mod4, 2 * i + 1, g_ffn, ffn_w_up[i], ffn_conv_w[i], ffn_conv_b[i],
                 ffn_w_down[i], final, tm, fc, final_norm=(i == depth - 1))
    return h
```

```python
import functools

import jax
import jax.numpy as jnp
from jax import lax
from jax.experimental import pallas as pl
from jax.experimental.pallas import tpu as pltpu

EPS = 1e-6
LANES = 128
SUBLANES = 8
BF16_ROWS = 16
VMEM_LIMIT = 56 * 1024 * 1024

S5_CHUNK = 4
S5_SEGMENTS = SUBLANES
S5_SCAN_UNROLL = 8

f32 = jnp.float32
bf16 = jnp.bfloat16


def _cparams(*sem):
    return pltpu.CompilerParams(dimension_semantics=sem, vmem_limit_bytes=VMEM_LIMIT)


def _resident(shape):
    nd = len(shape)
    return pl.BlockSpec(shape, lambda *_: (0,) * nd, pipeline_mode=pl.Buffered(1))


def _dot(a, b):
    return jnp.dot(a, b, preferred_element_type=f32)


def _dot_nt(a, b):
    return lax.dot_general(a, b, (((1,), (1,)), ((), ())), preferred_element_type=f32)


SUB_ROWS = 256


def _subtiles(tm):
    sub = min(SUB_ROWS, tm)
    assert tm % sub == 0
    return [slice(k * sub, (k + 1) * sub) for k in range(tm // sub)]


def _split3(x):
    hi = x.astype(bf16)
    r1 = x - hi.astype(f32)
    mid = r1.astype(bf16)
    lo = (r1 - mid.astype(f32)).astype(bf16)
    return hi, mid, lo


def _modulate(x, g, shift, scale):
    ms = jnp.mean(x * x, axis=-1, keepdims=True)
    return x * lax.rsqrt(ms + EPS) * g * (1.0 + scale) + shift


def _silu(x):
    return x * (1.0 / (1.0 + jnp.exp(-x)))


def _gelu_tanh(x):
    c = 0.7978845608028654
    return 0.5 * x * (1.0 + jnp.tanh(c * (x + 0.044715 * (x * x * x))))


def _adaln_kernel(ct_ref, w_ref, b_ref, o_ref):
    ct = ct_ref[...]
    cs = _silu(ct)
    w = w_ref[...]
    for b in range(ct.shape[1]):
        col = cs[:, b:b + 1]
        o_ref[b:b + 1, :] = jnp.sum(w * col, axis=0, keepdims=True) + b_ref[...]


def _adaln(c, ada_w, ada_b):
    B, D = c.shape
    L = ada_w.shape[0] * ada_w.shape[1]
    N = ada_w.shape[-1]
    tn = 512
    w = ada_w.reshape(L, D, N)
    bias = ada_b.reshape(L, 1, N)
    return pl.pallas_call(
        _adaln_kernel,
        grid=(L, N // tn),
        in_specs=[
            pl.BlockSpec((D, B), lambda l, j: (0, 0)),
            pl.BlockSpec((None, D, tn), lambda l, j: (l, 0, j)),
            pl.BlockSpec((None, 1, tn), lambda l, j: (l, 0, j)),
        ],
        out_specs=pl.BlockSpec((None, B, tn), lambda l, j: (l, 0, j)),
        out_shape=jax.ShapeDtypeStruct((L, B, N), f32),
        compiler_params=_cparams("parallel", "parallel"),
        name="adaln",
    )(c.T, w, bias)


def _mod_spec(D):
    return lambda l: pl.BlockSpec((None, None, 3, D), lambda b, i: (l, b, 0, 0))


def _s5_in_kernel(x_ref, mod_ref, g_ref, w_ref, u_ref):
    g, shift, scale = g_ref[...], mod_ref[0:1, :], mod_ref[1:2, :]
    for rows in _subtiles(x_ref.shape[0]):
        hn = _modulate(x_ref[rows, :], g, shift, scale)
        u_ref[rows, :] = _dot(hn.astype(bf16), w_ref[...])


def _s5_in(x, mod4, l, g, w_bf, tm):
    B, S, D = x.shape
    return pl.pallas_call(
        _s5_in_kernel,
        grid=(B, S // tm),
        in_specs=[
            pl.BlockSpec((None, tm, D), lambda b, i: (b, i, 0)),
            _mod_spec(D)(l),
            _resident((1, D)),
            _resident((D, D)),
        ],
        out_specs=pl.BlockSpec((None, tm, D), lambda b, i: (b, i, 0)),
        out_shape=jax.ShapeDtypeStruct((B, S, D), f32),
        compiler_params=_cparams("parallel", "parallel"),
        name="s5_in",
    )(x, mod4, g, w_bf)


def _s5_weights_kernel(lre_ref, lim_ref, ldt_ref, br_ref, bi_ref, cr_ref, ci_ref,
                       wcs_ref, wout_ref, wtz_ref, adec_ref, *, lc, cg, p_state):
    lre = lre_ref[...]
    lim = lim_ref[...]
    dt = jnp.exp(ldt_ref[...])
    nl = lre.shape[1]
    mag = jnp.exp(lre * dt)
    lam_r = mag * jnp.cos(lim * dt)
    lam_i = mag * jnp.sin(lim * dt)
    num_r = lam_r - 1.0
    den = lre * lre + lim * lim
    k_r = (num_r * lre + lam_i * lim) / den
    k_i = (lam_i * lre - num_r * lim) / den
    br = br_ref[...]
    bi = bi_ref[...]
    bb_r = k_r * br - k_i * bi
    bb_i = k_r * bi + k_i * br
    cr = cr_ref[...]
    ci = ci_ref[...]

    ngrp = nl // p_state
    rows = ngrp * cg
    row_grp = lax.broadcasted_iota(jnp.int32, (rows, nl), 0) // cg
    lane_grp = lax.broadcasted_iota(jnp.int32, (rows, nl), 1) // p_state
    own = row_grp == lane_grp

    def slab(xr, xi):
        tr = jnp.where(own, jnp.tile(xr, (ngrp, 1)), 0.0)
        ti = jnp.where(own, jnp.tile(xi, (ngrp, 1)), 0.0)
        return jnp.concatenate([tr, ti], axis=1)

    pw = [(jnp.ones_like(lam_r), jnp.zeros_like(lam_r))]
    for _ in range(lc):
        pr, pi = pw[-1]
        pw.append((pr * lam_r - pi * lam_i, pr * lam_i + pi * lam_r))
    adec_ref[0:1, :] = pw[lc][0]
    adec_ref[1:2, :] = pw[lc][1]

    for s in range(lc):
        pr, pi = pw[lc - 1 - s]
        wcs_ref[s * rows:(s + 1) * rows, :] = slab(pr * bb_r - pi * bb_i,
                                                   pr * bb_i + pi * bb_r).astype(bf16)

    wq = []
    for t in range(lc + 1):
        pr, pi = pw[t]
        wq.append(slab(cr * pr - ci * pi, -(cr * pi + ci * pr)))
    for r in range(lc):
        wout_ref[r * rows:(r + 1) * rows, :] = wq[r + 1].astype(bf16)

    bsl = _split3(slab(bb_r, bb_i))
    kt = []
    for t in range(lc):
        q = _split3(wq[t])
        acc = _dot_nt(bsl[0], q[0])
        acc += _dot_nt(bsl[0], q[1]) + _dot_nt(bsl[1], q[0])
        acc += _dot_nt(bsl[1], q[1]) + _dot_nt(bsl[0], q[2]) + _dot_nt(bsl[2], q[0])
        kt.append(acc)
    zero = jnp.zeros((rows, rows), f32)
    for s in range(lc):
        for r in range(lc):
            blk = kt[r - s] if r >= s else zero
            wtz_ref[s * rows:(s + 1) * rows, r * rows:(r + 1) * rows] = blk.astype(bf16)


def _s5_weights(lam_re, lam_im, log_dt, b_re, b_im, c_re, c_im, lc):
    G, P = lam_re.shape
    Cg = b_re.shape[-1]
    gb = LANES // Cg
    nb = G // gb
    nl = gb * P
    rows = gb * Cg
    lre = lam_re.reshape(nb, 1, nl)
    lim = lam_im.reshape(nb, 1, nl)
    ldt = jnp.repeat(log_dt, P).reshape(nb, 1, nl)
    brt = b_re.reshape(nb, gb, P, Cg).transpose(0, 3, 1, 2).reshape(nb, Cg, nl)
    bit = b_im.reshape(nb, gb, P, Cg).transpose(0, 3, 1, 2).reshape(nb, Cg, nl)
    crt = c_re.reshape(nb, gb, Cg, P).transpose(0, 2, 1, 3).reshape(nb, Cg, nl)
    cit = c_im.reshape(nb, gb, Cg, P).transpose(0, 2, 1, 3).reshape(nb, Cg, nl)
    vec = pl.BlockSpec((None, 1, nl), lambda k: (k, 0, 0))
    mat = pl.BlockSpec((None, Cg, nl), lambda k: (k, 0, 0))
    return pl.pallas_call(
        functools.partial(_s5_weights_kernel, lc=lc, cg=Cg, p_state=P),
        grid=(nb,),
        in_specs=[vec, vec, vec, mat, mat, mat, mat],
        out_specs=[
            pl.BlockSpec((None, lc * rows, 2 * nl), lambda k: (k, 0, 0)),
            pl.BlockSpec((None, lc * rows, 2 * nl), lambda k: (k, 0, 0)),
            pl.BlockSpec((None, lc * rows, lc * rows), lambda k: (k, 0, 0)),
            pl.BlockSpec((None, 2, nl), lambda k: (k, 0, 0)),
        ],
        out_shape=[
            jax.ShapeDtypeStruct((nb, lc * rows, 2 * nl), bf16),
            jax.ShapeDtypeStruct((nb, lc * rows, 2 * nl), bf16),
            jax.ShapeDtypeStruct((nb, lc * rows, lc * rows), bf16),
            jax.ShapeDtypeStruct((nb, 2, nl), f32),
        ],
        compiler_params=_cparams("parallel"),
        name="s5_weights",
    )(lre, lim, ldt, brt, bit, crt, cit)


def _s5_core_kernel(u_ref, wcs_ref, wout_ref, wtz_ref, adec_ref, d_ref, z_ref,
                    st_ref, y1_ref, y_ref, *, lc, nseg, seg_len):
    nslab = st_ref.shape[0]
    half = nslab // 2
    seg_tok = seg_len * lc

    def phase_a(seg, carry):
        tok0 = seg * seg_tok
        x = jnp.concatenate(
            [u_ref[pl.ds(tok0 + s, seg_len, stride=lc), :] for s in range(lc)], axis=1)
        xb = x.astype(bf16)
        e = _dot(xb, wcs_ref[...])
        for j in range(nslab):
            st_ref[j, pl.ds(seg, seg_len, stride=nseg), :] = e[:, j * LANES:(j + 1) * LANES]
        y1_ref[pl.ds(seg * seg_len, seg_len), :] = _dot(xb, wtz_ref[...])
        return carry

    for seg in range(nseg):
        phase_a(seg, 0)

    a_r = [jnp.broadcast_to(adec_ref[0:1, j * LANES:(j + 1) * LANES], (nseg, LANES))
           for j in range(half)]
    a_i = [jnp.broadcast_to(adec_ref[1:2, j * LANES:(j + 1) * LANES], (nseg, LANES))
           for j in range(half)]

    def step(i, hs, store):
        out = []
        rows = pl.ds(pl.multiple_of(i * nseg, nseg), nseg)
        for j in range(half):
            hr, hi = hs[2 * j], hs[2 * j + 1]
            er = st_ref[j, rows, :]
            ei = st_ref[half + j, rows, :]
            if store:
                st_ref[j, rows, :] = hr
                st_ref[half + j, rows, :] = hi
            out.append(a_r[j] * hr - a_i[j] * hi + er)
            out.append(a_r[j] * hi + a_i[j] * hr + ei)
        return tuple(out)

    zeros = tuple(jnp.zeros((nseg, LANES), f32) for _ in range(nslab))
    ends = lax.fori_loop(0, seg_len, lambda i, hs: step(i, hs, False), zeros,
                         unroll=S5_SCAN_UNROLL)

    starts = []
    for j in range(half):
        pr, pi = a_r[j][0:1, :], a_i[j][0:1, :]
        n = seg_len
        while n > 1:
            pr, pi = pr * pr - pi * pi, 2.0 * pr * pi
            n //= 2
        er, ei = ends[2 * j], ends[2 * j + 1]
        sr = [jnp.zeros((1, LANES), f32)]
        si = [jnp.zeros((1, LANES), f32)]
        for k in range(1, nseg):
            sr.append(pr * sr[-1] - pi * si[-1] + er[k - 1:k, :])
            si.append(pr * si[k - 1] + pi * sr[k - 1] + ei[k - 1:k, :])
        starts.append(jnp.concatenate(sr, axis=0))
        starts.append(jnp.concatenate(si, axis=0))
    lax.fori_loop(0, seg_len, lambda i, hs: step(i, hs, True), tuple(starts),
                  unroll=S5_SCAN_UNROLL)

    d = d_ref[...]

    def phase_c(seg):
        h = jnp.concatenate([st_ref[j, pl.ds(seg, seg_len, stride=nseg), :]
                             for j in range(nslab)], axis=1)
        y = _dot_nt(h.astype(bf16), wout_ref[...])
        y = y + y1_ref[pl.ds(seg * seg_len, seg_len), :]
        tok0 = seg * seg_tok
        for r in range(lc):
            y_ref[pl.ds(tok0 + r, seg_len, stride=lc), :] = y[:, r * LANES:(r + 1) * LANES]

    def phase_d(seg):
        rows = pl.ds(seg * seg_tok, seg_tok)
        z_ref[rows, :] = _gelu_tanh(y_ref[rows, :] + d * u_ref[rows, :]).astype(z_ref.dtype)

    phase_c(0)
    for seg in range(nseg):
        if seg + 1 < nseg:
            phase_c(seg + 1)
        phase_d(seg)


def _s5_core(u, wcs, wout, wtz, adec, d_skip, lc):
    B, S, D = u.shape
    nb = D // LANES
    nseg = S5_SEGMENTS
    m = S // lc
    seg_len = m // nseg
    assert seg_len * nseg * lc == S and seg_len & (seg_len - 1) == 0
    nslab = wcs.shape[-1] // LANES
    kw = wcs.shape[1]
    kern = functools.partial(_s5_core_kernel, lc=lc, nseg=nseg, seg_len=seg_len)
    return pl.pallas_call(
        kern,
        grid=(nb, B),
        in_specs=[
            pl.BlockSpec((None, S, LANES), lambda k, b: (b, 0, k)),
            pl.BlockSpec((None, kw, nslab * LANES), lambda k, b: (k, 0, 0)),
            pl.BlockSpec((None, kw, nslab * LANES), lambda k, b: (k, 0, 0)),
            pl.BlockSpec((None, kw, kw), lambda k, b: (k, 0, 0)),
            pl.BlockSpec((None, 2, adec.shape[-1]), lambda k, b: (k, 0, 0)),
            pl.BlockSpec((1, LANES), lambda k, b: (0, k)),
        ],
        out_specs=pl.BlockSpec((None, S, LANES), lambda k, b: (b, 0, k)),
        out_shape=jax.ShapeDtypeStruct((B, S, D), bf16),
        scratch_shapes=[
            pltpu.VMEM((nslab, nseg * seg_len, LANES), f32),
            pltpu.VMEM((m, kw), f32),
            pltpu.VMEM((S, LANES), f32),
        ],
        compiler_params=_cparams("parallel", "parallel"),
        name="s5_core",
    )(u, wcs, wout, wtz, adec, d_skip)


def _s5_out_kernel(z_ref, x_ref, mod_ref, wg_ref, wo_ref, h_ref):
    subs = _subtiles(z_ref.shape[0])
    gate = mod_ref[2:3, :]

    def glu_logits(rows):
        return _dot(z_ref[rows, :], wg_ref[...])

    g_next = glu_logits(subs[0])
    for k, rows in enumerate(subs):
        g = g_next
        if k + 1 < len(subs):
            g_next = glu_logits(subs[k + 1])
        z2 = z_ref[rows, :].astype(f32) * (1.0 / (1.0 + jnp.exp(-g)))
        m = _dot(z2.astype(bf16), wo_ref[...])
        h_ref[rows, :] = x_ref[rows, :] + gate * m


def _s5_out(z, x, mod4, l, wg_bf, wo_bf, tm):
    B, S, D = x.shape
    row = pl.BlockSpec((None, tm, D), lambda b, i: (b, i, 0))
    return pl.pallas_call(
        _s5_out_kernel,
        grid=(B, S // tm),
        in_specs=[row, row, _mod_spec(D)(l), _resident((D, D)), _resident((D, D))],
        out_specs=row,
        out_shape=jax.ShapeDtypeStruct((B, S, D), f32),
        compiler_params=_cparams("parallel", "parallel"),
        name="s5_out",
    )(z, x, mod4, wg_bf, wo_bf)


def _ffn_kernel(h_ref, halo_ref, mod_ref, g_ref, wa_ref, wb_ref, wd_ref, cw_ref, cb_ref,
                fg_ref, o_ref, hn_ref, a_ref, b_ref, gt_ref, *, taps, final_norm):
    i = pl.program_id(1)
    tm = h_ref.shape[0]
    hb = halo_ref.shape[0]
    nc = wa_ref.shape[0]
    g, shift, scale = g_ref[...], mod_ref[0:1, :], mod_ref[1:2, :]
    x = h_ref[...]
    keep = (i > 0).astype(f32)
    hn_ref[0:hb, :] = (_modulate(halo_ref[...], g, shift, scale) * keep).astype(bf16)
    hn_ref[hb:, :] = _modulate(x, g, shift, scale).astype(bf16)

    def up(j, slot):
        a_ref[slot] = _dot(hn_ref[...], wa_ref[j])
        b_ref[slot] = _dot(hn_ref[hb:, :], wb_ref[j])

    def gate(j, slot):
        cw = cw_ref[j]
        ac = cb_ref[j]
        for t in range(taps):
            off = hb - (taps - 1) + t
            ac = ac + cw[t:t + 1, :] * a_ref[slot, off:off + tm, :]
        gt_ref[j] = (_silu(ac) * b_ref[slot]).astype(bf16)

    up(0, 0)
    for j in range(nc):
        if j + 1 < nc:
            up(j + 1, (j + 1) % 2)
        gate(j, j % 2)
    acc = _dot(gt_ref[0], wd_ref[0])
    for j in range(1, nc):
        acc += _dot(gt_ref[j], wd_ref[j])
    out = x + mod_ref[2:3, :] * acc
    if final_norm:
        ms = jnp.mean(out * out, axis=-1, keepdims=True)
        out = out * lax.rsqrt(ms + EPS) * fg_ref[...]
    o_ref[...] = out


def _ffn(h, mod4, l, g, w_up, conv_w, conv_b, w_down, final_g, tm, fc, final_norm):
    B, S, D = h.shape
    F = w_down.shape[0]
    taps = conv_w.shape[0]
    nc = F // fc
    hb = BF16_ROWS
    wa = w_up[:, :F].reshape(D, nc, fc).transpose(1, 0, 2).astype(bf16)
    wb = w_up[:, F:].reshape(D, nc, fc).transpose(1, 0, 2).astype(bf16)
    wd = w_down.reshape(nc, fc, D).astype(bf16)
    cw = conv_w.reshape(taps, nc, fc).transpose(1, 0, 2)
    cb = conv_b.reshape(nc, 1, fc)
    per = tm // hb
    kern = functools.partial(_ffn_kernel, taps=taps, final_norm=final_norm)
    return pl.pallas_call(
        kern,
        grid=(B, S // tm),
        in_specs=[
            pl.BlockSpec((None, tm, D), lambda b, i: (b, i, 0)),
            pl.BlockSpec((None, hb, D), lambda b, i: (b, jnp.maximum(i * per - 1, 0), 0)),
            _mod_spec(D)(l),
            _resident((1, D)),
            _resident((nc, D, fc)),
            _resident((nc, D, fc)),
            _resident((nc, fc, D)),
            _resident((nc, taps, fc)),
            _resident((nc, 1, fc)),
            _resident((1, D)),
        ],
        out_specs=pl.BlockSpec((None, tm, D), lambda b, i: (b, i, 0)),
        out_shape=jax.ShapeDtypeStruct((B, S, D), f32),
        scratch_shapes=[
            pltpu.VMEM((hb + tm, D), bf16),
            pltpu.VMEM((2, hb + tm, fc), f32),
            pltpu.VMEM((2, tm, fc), f32),
            pltpu.VMEM((nc, tm, fc), bf16),
        ],
        compiler_params=_cparams("parallel", "parallel"),
        name="ffn",
    )(h, h, mod4, g, wa, wb, wd, cw, cb, final_g)


LOG2E = 1.4426950408889634


def _fox_in_kernel(h_ref, mod_ref, g_ref, wqt_ref, wk_ref, wvt_ref, wf_ref, bf_ref,
                   qt_ref, k_ref, vt_ref, fc_ref, carry_ref, *, q_scale, ts):
    i = pl.program_id(1)
    tm, D = h_ref.shape

    @pl.when(i == 0)
    def _():
        carry_ref[...] = jnp.zeros_like(carry_ref)

    g, shift, scale = g_ref[...], mod_ref[0:1, :], mod_ref[1:2, :]
    subs = _subtiles(tm)
    sub = subs[0].stop
    r = lax.broadcasted_iota(jnp.int32, (sub, sub), 0)
    c = lax.broadcasted_iota(jnp.int32, (sub, sub), 1)
    tri = (c <= r).astype(bf16)
    carry = carry_ref[...]
    for rows in subs:
        hn = _modulate(h_ref[rows, :], g, shift, scale).astype(bf16)
        k_ref[rows, :] = _dot(hn, wk_ref[...]).astype(bf16)
        qt_ref[:, rows] = (_dot_nt(wqt_ref[...], hn) * q_scale).astype(bf16)
        vt = _dot_nt(wvt_ref[...], hn).astype(bf16)
        for cc in range(sub // ts):
            vt_ref[rows.start // ts + cc] = vt[:, cc * ts:(cc + 1) * ts]

        fl = _dot(hn, wf_ref[...]) + bf_ref[...]
        lf = jnp.minimum(fl, 0.0) - jnp.log(1.0 + jnp.exp(-jnp.abs(fl)))
        p0, p1, p2 = _split3(lf)
        cs = _dot(tri, p0) + _dot(tri, p1) + _dot(tri, p2) + carry
        fc_ref[rows, :] = cs * LOG2E
        carry = cs[sub - 1:sub, :]
    carry_ref[...] = carry


def _fox_in(h, mod4, l, g, wqt_bf, wk_bf, wvt_bf, wf_bf, bf_pad, q_scale, tm, ts):
    B, S, D = h.shape
    return pl.pallas_call(
        functools.partial(_fox_in_kernel, q_scale=q_scale, ts=ts),
        grid=(B, S // tm),
        in_specs=[
            pl.BlockSpec((None, tm, D), lambda b, i: (b, i, 0)),
            _mod_spec(D)(l),
            _resident((1, D)),
            _resident((D, D)),
            _resident((D, D)),
            _resident((D, D)),
            _resident((D, LANES)),
            _resident((1, LANES)),
        ],
        out_specs=[
            pl.BlockSpec((None, D, tm), lambda b, i: (b, 0, i)),
            pl.BlockSpec((None, tm, D), lambda b, i: (b, i, 0)),
            pl.BlockSpec((None, tm // ts, D, ts), lambda b, i: (b, i, 0, 0)),
            pl.BlockSpec((None, tm, LANES), lambda b, i: (b, i, 0)),
        ],
        out_shape=[
            jax.ShapeDtypeStruct((B, D, S), bf16),
            jax.ShapeDtypeStruct((B, S, D), bf16),
            jax.ShapeDtypeStruct((B, S // ts, D, ts), bf16),
            jax.ShapeDtypeStruct((B, S, LANES), f32),
        ],
        scratch_shapes=[pltpu.VMEM((1, LANES), f32)],
        compiler_params=_cparams("parallel", "arbitrary"),
        name="fox_in",
    )(h, mod4, g, wqt_bf, wk_bf, wvt_bf, wf_bf, bf_pad)


NEG_BIG = -1e30
ATTN_LOOKAHEAD = 3


def _sublane_allreduce(x, op):
    for sh in (4, 2, 1):
        x = op(x, pltpu.roll(x, sh, axis=0))
    return x


def _tree(xs, op):
    while len(xs) > 1:
        xs = [op(xs[i], xs[i + 1]) if i + 1 < len(xs) else xs[i] for i in range(0, len(xs), 2)]
    return xs[0]


def _attn_kernel(qt_ref, k_ref, vt_ref, fk_ref, o_ref, fkb_ref, s_ref, *, ts, dh):
    qi = pl.program_id(2)
    tq = qt_ref.shape[1]
    heads = LANES // dh
    spq = tq // ts

    @pl.when(qi == 0)
    def _():
        def fill(a, carry):
            for e in range(heads):
                row = fk_ref[e, pl.ds(a, 1), :]
                rows = pl.ds(pl.multiple_of(a * ts, ts), ts)
                fkb_ref[e, rows, :] = jnp.broadcast_to(row, (LANES, ts)).T
            return carry

        lax.fori_loop(0, k_ref.shape[0] // ts, fill, 0)

    qt = qt_ref[...]
    rowi = lax.broadcasted_iota(jnp.int32, qt.shape, 0)
    qts = [jnp.where((rowi // dh) == e, qt, jnp.zeros_like(qt)) for e in range(heads)]
    nrt = ts // SUBLANES

    def issue(j):
        jd, slot = j
        ks = pl.ds(pl.multiple_of(jd * ts, ts), ts)
        kb = k_ref[ks, :]
        for e in range(heads):
            fkb = fkb_ref[e, ks, :]
            s_ref[slot, e] = _dot(kb, qts[e]) - jnp.concatenate([fkb] * (tq // LANES), axis=1)

    ones_rows = (lax.broadcasted_iota(jnp.int32, (BF16_ROWS, ts), 0) == 0).astype(bf16)
    acc_rows = dh + SUBLANES

    def absorb(j, carry, key_off):
        jd, slot = j
        vtb = vt_ref[jd]
        out = []
        for e in range(heads):
            m, acc = carry[2 * e:2 * e + 2]
            s = s_ref[slot, e]
            if key_off is not None:
                kidx = lax.broadcasted_iota(jnp.int32, (ts, tq), 0) + key_off
                qidx = lax.broadcasted_iota(jnp.int32, (ts, tq), 1)
                s = jnp.where(kidx <= qidx, s, NEG_BIG)
            tiles = [s[r * SUBLANES:(r + 1) * SUBLANES, :] for r in range(nrt)]
            m_new = jnp.maximum(m, _sublane_allreduce(_tree(tiles, jnp.maximum), jnp.maximum))
            alpha = jnp.exp2(m - m_new)
            p = jnp.concatenate([jnp.exp2(t - m_new) for t in tiles], axis=0).astype(bf16)
            lhs = jnp.concatenate([vtb[e * dh:(e + 1) * dh, :], ones_rows], axis=0)
            pv = _dot(lhs, p)[:acc_rows, :]
            acc = jnp.concatenate([alpha] * (acc_rows // SUBLANES), axis=0) * acc + pv
            out += [m_new, acc]
        return tuple(out)

    la = ATTN_LOOKAHEAD
    assert la < spq

    def strip_id(j0, d):
        return (j0 + d, d % spq)

    def group(j0, carry, masked):
        last = None if not masked else spq
        for d in range(spq):
            if last is None or d + la < last:
                issue(strip_id(j0, d + la))
            carry = absorb(strip_id(j0, d), carry, d * ts if masked else None)
        return carry

    for d in range(la):
        issue(strip_id(0, d))
    init = []
    for e in range(heads):
        init += [jnp.full((SUBLANES, tq), NEG_BIG, f32), jnp.zeros((acc_rows, tq), f32)]
    def pair(t, c):
        return group((2 * t + 1) * spq, group(2 * t * spq, c, False), False)

    carry = lax.fori_loop(0, qi // 2, pair, tuple(init))
    carry = lax.fori_loop(qi - qi % 2, qi, lambda g, c: group(g * spq, c, False), carry)
    carry = group(qi * spq, carry, True)
    outs = []
    for e in range(heads):
        acc = carry[2 * e + 1]
        l = jnp.broadcast_to(acc[dh:dh + 1, :], (dh, tq))
        outs.append(acc[:dh, :] / l)
    o_ref[...] = jnp.concatenate(outs, axis=0).T.astype(o_ref.dtype)


def _attn(qt, k, vt, fk, n_heads, tq):
    B, D, S = qt.shape
    ts = vt.shape[-1]
    dh = D // n_heads
    heads = LANES // dh
    nhp = D // LANES
    kern = functools.partial(_attn_kernel, ts=ts, dh=dh)
    return pl.pallas_call(
        kern,
        grid=(B, nhp, S // tq),
        in_specs=[
            pl.BlockSpec((None, LANES, tq), lambda b, h, i: (b, h, i)),
            pl.BlockSpec((None, S, LANES), lambda b, h, i: (b, 0, h)),
            pl.BlockSpec((None, S // ts, LANES, ts), lambda b, h, i: (b, 0, h, 0)),
            pl.BlockSpec((None, heads, S // ts, ts), lambda b, h, i: (b, h, 0, 0)),
        ],
        out_specs=pl.BlockSpec((None, tq, LANES), lambda b, h, i: (b, i, h)),
        out_shape=jax.ShapeDtypeStruct((B, S, D), bf16),
        scratch_shapes=[pltpu.VMEM((heads, S, LANES), f32),
                        pltpu.VMEM((tq // ts, heads, ts, tq), f32)],
        compiler_params=_cparams("parallel", "parallel", "arbitrary"),
        name="attn",
    )(qt, k, vt, fk)


def _fox_out_kernel(o_ref, h_ref, mod_ref, w_ref, out_ref):
    gate = mod_ref[2:3, :]
    for rows in _subtiles(o_ref.shape[0]):
        out_ref[rows, :] = h_ref[rows, :] + gate * _dot(o_ref[rows, :], w_ref[...])


def _fox_out(o, h, mod4, l, w_bf, tm):
    B, S, D = h.shape
    row = pl.BlockSpec((None, tm, D), lambda b, i: (b, i, 0))
    return pl.pallas_call(
        _fox_out_kernel,
        grid=(B, S // tm),
        in_specs=[row, row, _mod_spec(D)(l), _resident((D, D))],
        out_specs=row,
        out_shape=jax.ShapeDtypeStruct((B, S, D), f32),
        compiler_params=_cparams("parallel", "parallel"),
        name="fox_out",
    )(o, h, mod4, w_bf)


def kernel(x, c, norm_g, ada_w, ada_b, s5_w_in, s5_lam_re, s5_lam_im, s5_log_dt, s5_b_re, s5_b_im, s5_c_re, s5_c_im, s5_d, s5_w_glu, s5_w_out, fox_w_in, fox_b_f, fox_w_out, ffn_w_up, ffn_conv_w, ffn_conv_b, ffn_w_down, final_g):
    B, S, D = x.shape
    depth = norm_g.shape[0]
    n_heads = fox_b_f.shape[-1]
    tm = min(512, S)
    tmm = min(1024, S)
    fc = 256
    final = final_g.reshape(1, D)

    mod = _adaln(c, ada_w, ada_b)
    mod4 = mod.reshape(depth * 2, B, 3, D)

    h = x
    for i in range(depth):
        j = i // 2
        g_mix = norm_g[i, 0].reshape(1, D)
        g_ffn = norm_g[i, 1].reshape(1, D)
        if i % 2 == 0:
            u = _s5_in(h, mod4, 2 * i, g_mix, s5_w_in[j].astype(bf16), tmm)
            wcs, wout, wtz, adec = _s5_weights(s5_lam_re[j], s5_lam_im[j], s5_log_dt[j],
                                               s5_b_re[j], s5_b_im[j], s5_c_re[j], s5_c_im[j],
                                               S5_CHUNK)
            z = _s5_core(u, wcs, wout, wtz, adec, s5_d[j].reshape(1, D), S5_CHUNK)
            h = _s5_out(z, h, mod4, 2 * i, s5_w_glu[j].astype(bf16),
                        s5_w_out[j].astype(bf16), tmm)
        else:
            w_in = fox_w_in[j]
            wqt = w_in[:, :D].T.astype(bf16)
            wk = w_in[:, D:2 * D].astype(bf16)
            wvt = w_in[:, 2 * D:3 * D].T.astype(bf16)
            wf = jnp.pad(w_in[:, 3 * D:], ((0, 0), (0, LANES - n_heads))).astype(bf16)
            bfp = jnp.pad(fox_b_f[j], (0, LANES - n_heads)).reshape(1, LANES)
            ts = LANES
            q_scale = float((D // n_heads) ** -0.5 * LOG2E)
            qt, k, vt, fcum = _fox_in(h, mod4, 2 * i, g_mix, wqt, wk, wvt, wf, bfp,
                                      q_scale, tmm, ts)
            fk = fcum[:, :, :n_heads].transpose(0, 2, 1).reshape(B, n_heads, S // ts, ts)
            o = _attn(qt, k, vt, fk, n_heads, tq=min(512, S))
            h = _fox_out(o, h, mod4, 2 * i, fox_w_out[j].astype(bf16), tmm)
        h = _ffn(h, mod4, 2 * i + 1, g_ffn, ffn_w_up[i], ffn_conv_w[i], ffn_conv_b[i],
                 ffn_w_down[i], final, tm, fc, final_norm=(i == depth - 1))
    return h
```

```python
import functools

import jax
import jax.numpy as jnp
from jax import lax
from jax.experimental import pallas as pl
from jax.experimental.pallas import tpu as pltpu

EPS = 1e-6
LANES = 128
SUBLANES = 8
BF16_ROWS = 16
VMEM_LIMIT = 56 * 1024 * 1024

S5_CHUNK = 4
S5_SEGMENTS = SUBLANES
S5_SCAN_UNROLL = 8

f32 = jnp.float32
bf16 = jnp.bfloat16


def _cparams(*sem):
    return pltpu.CompilerParams(dimension_semantics=sem, vmem_limit_bytes=VMEM_LIMIT)


def _resident(shape):
    nd = len(shape)
    return pl.BlockSpec(shape, lambda *_: (0,) * nd, pipeline_mode=pl.Buffered(1))


def _dot(a, b):
    return jnp.dot(a, b, preferred_element_type=f32)


def _dot_nt(a, b):
    return lax.dot_general(a, b, (((1,), (1,)), ((), ())), preferred_element_type=f32)


SUB_ROWS = 256


def _subtiles(tm):
    sub = min(SUB_ROWS, tm)
    assert tm % sub == 0
    return [slice(k * sub, (k + 1) * sub) for k in range(tm // sub)]


def _split3(x):
    hi = x.astype(bf16)
    r1 = x - hi.astype(f32)
    mid = r1.astype(bf16)
    lo = (r1 - mid.astype(f32)).astype(bf16)
    return hi, mid, lo


def _modulate(x, g, shift, scale):
    ms = jnp.mean(x * x, axis=-1, keepdims=True)
    return x * lax.rsqrt(ms + EPS) * g * (1.0 + scale) + shift


def _silu(x):
    return x * (1.0 / (1.0 + jnp.exp(-x)))


def _gelu_tanh(x):
    c = 0.7978845608028654
    return 0.5 * x * (1.0 + jnp.tanh(c * (x + 0.044715 * (x * x * x))))


def _adaln_kernel(ct_ref, w_ref, b_ref, o_ref):
    ct = ct_ref[...]
    cs = _silu(ct)
    w = w_ref[...]
    for b in range(ct.shape[1]):
        col = cs[:, b:b + 1]
        o_ref[b:b + 1, :] = jnp.sum(w * col, axis=0, keepdims=True) + b_ref[...]


def _adaln(c, ada_w, ada_b):
    B, D = c.shape
    L = ada_w.shape[0] * ada_w.shape[1]
    N = ada_w.shape[-1]
    tn = 512
    w = ada_w.reshape(L, D, N)
    bias = ada_b.reshape(L, 1, N)
    return pl.pallas_call(
        _adaln_kernel,
        grid=(L, N // tn),
        in_specs=[
            pl.BlockSpec((D, B), lambda l, j: (0, 0)),
            pl.BlockSpec((None, D, tn), lambda l, j: (l, 0, j)),
            pl.BlockSpec((None, 1, tn), lambda l, j: (l, 0, j)),
        ],
        out_specs=pl.BlockSpec((None, B, tn), lambda l, j: (l, 0, j)),
        out_shape=jax.ShapeDtypeStruct((L, B, N), f32),
        compiler_params=_cparams("parallel", "parallel"),
        name="adaln",
    )(c.T, w, bias)


def _mod_spec(D):
    return lambda l: pl.BlockSpec((None, None, 3, D), lambda b, i: (l, b, 0, 0))


def _s5_in_kernel(x_ref, mod_ref, g_ref, w_ref, u_ref):
    g, shift, scale = g_ref[...], mod_ref[0:1, :], mod_ref[1:2, :]
    for rows in _subtiles(x_ref.shape[0]):
        hn = _modulate(x_ref[rows, :], g, shift, scale)
        u = _dot(hn.astype(bf16), w_ref[...])
        for k in range(u_ref.shape[0]):
            u_ref[k, rows, :] = u[:, k * LANES:(k + 1) * LANES]


def _s5_in(x, mod4, l, g, w_bf, tm):
    B, S, D = x.shape
    return pl.pallas_call(
        _s5_in_kernel,
        grid=(B, S // tm),
        in_specs=[
            pl.BlockSpec((None, tm, D), lambda b, i: (b, i, 0)),
            _mod_spec(D)(l),
            _resident((1, D)),
            _resident((D, D)),
        ],
        out_specs=pl.BlockSpec((None, D // LANES, tm, LANES), lambda b, i: (b, 0, i, 0)),
        out_shape=jax.ShapeDtypeStruct((B, D // LANES, S, LANES), f32),
        compiler_params=_cparams("parallel", "parallel"),
        name="s5_in",
    )(x, mod4, g, w_bf)


def _s5_weights_kernel(lre_ref, lim_ref, ldt_ref, br_ref, bi_ref, cr_ref, ci_ref,
                       wcs_ref, wout_ref, wtz_ref, adec_ref, *, lc, cg, p_state):
    lre = lre_ref[...]
    lim = lim_ref[...]
    dt = jnp.exp(ldt_ref[...])
    nl = lre.shape[1]
    mag = jnp.exp(lre * dt)
    lam_r = mag * jnp.cos(lim * dt)
    lam_i = mag * jnp.sin(lim * dt)
    num_r = lam_r - 1.0
    den = lre * lre + lim * lim
    k_r = (num_r * lre + lam_i * lim) / den
    k_i = (lam_i * lre - num_r * lim) / den
    br = br_ref[...]
    bi = bi_ref[...]
    bb_r = k_r * br - k_i * bi
    bb_i = k_r * bi + k_i * br
    cr = cr_ref[...]
    ci = ci_ref[...]

    ngrp = nl // p_state
    rows = ngrp * cg
    row_grp = lax.broadcasted_iota(jnp.int32, (rows, nl), 0) // cg
    lane_grp = lax.broadcasted_iota(jnp.int32, (rows, nl), 1) // p_state
    own = row_grp == lane_grp

    def slab(xr, xi):
        tr = jnp.where(own, jnp.tile(xr, (ngrp, 1)), 0.0)
        ti = jnp.where(own, jnp.tile(xi, (ngrp, 1)), 0.0)
        return jnp.concatenate([tr, ti], axis=1)

    pw = [(jnp.ones_like(lam_r), jnp.zeros_like(lam_r))]
    for _ in range(lc):
        pr, pi = pw[-1]
        pw.append((pr * lam_r - pi * lam_i, pr * lam_i + pi * lam_r))
    adec_ref[0:1, :] = pw[lc][0]
    adec_ref[1:2, :] = pw[lc][1]

    for s in range(lc):
        pr, pi = pw[lc - 1 - s]
        wcs_ref[s * rows:(s + 1) * rows, :] = slab(pr * bb_r - pi * bb_i,
                                                   pr * bb_i + pi * bb_r).astype(bf16)

    wq = []
    for t in range(lc + 1):
        pr, pi = pw[t]
        wq.append(slab(cr * pr - ci * pi, -(cr * pi + ci * pr)))
    for r in range(lc):
        wout_ref[r * rows:(r + 1) * rows, :] = wq[r + 1].astype(bf16)

    bsl = _split3(slab(bb_r, bb_i))
    kt = []
    for t in range(lc):
        q = _split3(wq[t])
        acc = _dot_nt(bsl[0], q[0])
        acc += _dot_nt(bsl[0], q[1]) + _dot_nt(bsl[1], q[0])
        acc += _dot_nt(bsl[1], q[1]) + _dot_nt(bsl[0], q[2]) + _dot_nt(bsl[2], q[0])
        kt.append(acc)
    zero = jnp.zeros((rows, rows), f32)
    for s in range(lc):
        for r in range(lc):
            blk = kt[r - s] if r >= s else zero
            wtz_ref[s * rows:(s + 1) * rows, r * rows:(r + 1) * rows] = blk.astype(bf16)


def _s5_weights(lam_re, lam_im, log_dt, b_re, b_im, c_re, c_im, lc):
    G, P = lam_re.shape
    Cg = b_re.shape[-1]
    gb = LANES // Cg
    nb = G // gb
    nl = gb * P
    rows = gb * Cg
    lre = lam_re.reshape(nb, 1, nl)
    lim = lam_im.reshape(nb, 1, nl)
    ldt = jnp.repeat(log_dt, P).reshape(nb, 1, nl)
    brt = b_re.reshape(nb, gb, P, Cg).transpose(0, 3, 1, 2).reshape(nb, Cg, nl)
    bit = b_im.reshape(nb, gb, P, Cg).transpose(0, 3, 1, 2).reshape(nb, Cg, nl)
    crt = c_re.reshape(nb, gb, Cg, P).transpose(0, 2, 1, 3).reshape(nb, Cg, nl)
    cit = c_im.reshape(nb, gb, Cg, P).transpose(0, 2, 1, 3).reshape(nb, Cg, nl)
    vec = pl.BlockSpec((None, 1, nl), lambda k: (k, 0, 0))
    mat = pl.BlockSpec((None, Cg, nl), lambda k: (k, 0, 0))
    return pl.pallas_call(
        functools.partial(_s5_weights_kernel, lc=lc, cg=Cg, p_state=P),
        grid=(nb,),
        in_specs=[vec, vec, vec, mat, mat, mat, mat],
        out_specs=[
            pl.BlockSpec((None, lc * rows, 2 * nl), lambda k: (k, 0, 0)),
            pl.BlockSpec((None, lc * rows, 2 * nl), lambda k: (k, 0, 0)),
            pl.BlockSpec((None, lc * rows, lc * rows), lambda k: (k, 0, 0)),
            pl.BlockSpec((None, 2, nl), lambda k: (k, 0, 0)),
        ],
        out_shape=[
            jax.ShapeDtypeStruct((nb, lc * rows, 2 * nl), bf16),
            jax.ShapeDtypeStruct((nb, lc * rows, 2 * nl), bf16),
            jax.ShapeDtypeStruct((nb, lc * rows, lc * rows), bf16),
            jax.ShapeDtypeStruct((nb, 2, nl), f32),
        ],
        compiler_params=_cparams("parallel"),
        name="s5_weights",
    )(lre, lim, ldt, brt, bit, crt, cit)


def _s5_core_kernel(u_ref, wcs_ref, wout_ref, wtz_ref, adec_ref, d_ref, z_ref,
                    st_ref, y1_ref, y_ref, *, lc, nseg, seg_len):
    nslab = st_ref.shape[0]
    half = nslab // 2
    seg_tok = seg_len * lc

    def phase_a(seg, carry):
        tok0 = seg * seg_tok
        x = jnp.concatenate(
            [u_ref[pl.ds(tok0 + s, seg_len, stride=lc), :] for s in range(lc)], axis=1)
        xb = x.astype(bf16)
        e = _dot(xb, wcs_ref[...])
        for j in range(nslab):
            st_ref[j, pl.ds(seg, seg_len, stride=nseg), :] = e[:, j * LANES:(j + 1) * LANES]
        y1_ref[pl.ds(seg * seg_len, seg_len), :] = _dot(xb, wtz_ref[...])
        return carry

    for seg in range(nseg):
        phase_a(seg, 0)

    a_r = [jnp.broadcast_to(adec_ref[0:1, j * LANES:(j + 1) * LANES], (nseg, LANES))
           for j in range(half)]
    a_i = [jnp.broadcast_to(adec_ref[1:2, j * LANES:(j + 1) * LANES], (nseg, LANES))
           for j in range(half)]

    def step(i, hs, store):
        out = []
        rows = pl.ds(pl.multiple_of(i * nseg, nseg), nseg)
        for j in range(half):
            hr, hi = hs[2 * j], hs[2 * j + 1]
            er = st_ref[j, rows, :]
            ei = st_ref[half + j, rows, :]
            if store:
                st_ref[j, rows, :] = hr
                st_ref[half + j, rows, :] = hi
            out.append(a_r[j] * hr - a_i[j] * hi + er)
            out.append(a_r[j] * hi + a_i[j] * hr + ei)
        return tuple(out)

    zeros = tuple(jnp.zeros((nseg, LANES), f32) for _ in range(nslab))
    ends = lax.fori_loop(0, seg_len, lambda i, hs: step(i, hs, False), zeros,
                         unroll=S5_SCAN_UNROLL)

    starts = []
    for j in range(half):
        pr, pi = a_r[j][0:1, :], a_i[j][0:1, :]
        n = seg_len
        while n > 1:
            pr, pi = pr * pr - pi * pi, 2.0 * pr * pi
            n //= 2
        er, ei = ends[2 * j], ends[2 * j + 1]
        sr = [jnp.zeros((1, LANES), f32)]
        si = [jnp.zeros((1, LANES), f32)]
        for k in range(1, nseg):
            sr.append(pr * sr[-1] - pi * si[-1] + er[k - 1:k, :])
            si.append(pr * si[k - 1] + pi * sr[k - 1] + ei[k - 1:k, :])
        starts.append(jnp.concatenate(sr, axis=0))
        starts.append(jnp.concatenate(si, axis=0))
    lax.fori_loop(0, seg_len, lambda i, hs: step(i, hs, True), tuple(starts),
                  unroll=S5_SCAN_UNROLL)

    d = d_ref[...]

    def phase_c(seg):
        h = jnp.concatenate([st_ref[j, pl.ds(seg, seg_len, stride=nseg), :]
                             for j in range(nslab)], axis=1)
        y = _dot_nt(h.astype(bf16), wout_ref[...])
        y = y + y1_ref[pl.ds(seg * seg_len, seg_len), :]
        tok0 = seg * seg_tok
        for r in range(lc):
            y_ref[pl.ds(tok0 + r, seg_len, stride=lc), :] = y[:, r * LANES:(r + 1) * LANES]

    def phase_d(seg):
        rows = pl.ds(seg * seg_tok, seg_tok)
        z_ref[rows, :] = _gelu_tanh(y_ref[rows, :] + d * u_ref[rows, :]).astype(z_ref.dtype)

    phase_c(0)
    for seg in range(nseg):
        if seg + 1 < nseg:
            phase_c(seg + 1)
        phase_d(seg)


def _s5_core(u, wcs, wout, wtz, adec, d_skip, lc):
    B, nb, S, _ = u.shape
    nseg = S5_SEGMENTS
    m = S // lc
    seg_len = m // nseg
    assert seg_len * nseg * lc == S and seg_len & (seg_len - 1) == 0
    nslab = wcs.shape[-1] // LANES
    kw = wcs.shape[1]
    kern = functools.partial(_s5_core_kernel, lc=lc, nseg=nseg, seg_len=seg_len)
    return pl.pallas_call(
        kern,
        grid=(nb, B),
        in_specs=[
            pl.BlockSpec((None, None, S, LANES), lambda k, b: (b, k, 0, 0)),
            pl.BlockSpec((None, kw, nslab * LANES), lambda k, b: (k, 0, 0)),
            pl.BlockSpec((None, kw, nslab * LANES), lambda k, b: (k, 0, 0)),
            pl.BlockSpec((None, kw, kw), lambda k, b: (k, 0, 0)),
            pl.BlockSpec((None, 2, adec.shape[-1]), lambda k, b: (k, 0, 0)),
            pl.BlockSpec((1, LANES), lambda k, b: (0, k)),
        ],
        out_specs=pl.BlockSpec((None, None, S, LANES), lambda k, b: (b, k, 0, 0)),
        out_shape=jax.ShapeDtypeStruct((B, nb, S, LANES), bf16),
        scratch_shapes=[
            pltpu.VMEM((nslab, nseg * seg_len, LANES), f32),
            pltpu.VMEM((m, kw), f32),
            pltpu.VMEM((S, LANES), f32),
        ],
        compiler_params=_cparams("parallel", "parallel"),
        name="s5_core",
    )(u, wcs, wout, wtz, adec, d_skip)


def _s5_out_kernel(z_ref, x_ref, mod_ref, wg_ref, wo_ref, h_ref):
    subs = _subtiles(x_ref.shape[0])
    gate = mod_ref[2:3, :]

    def z_rows(rows):
        return jnp.concatenate([z_ref[k, rows, :] for k in range(z_ref.shape[0])], axis=1)

    def glu_logits(rows):
        return _dot(z_rows(rows), wg_ref[...])

    g_next = glu_logits(subs[0])
    for k, rows in enumerate(subs):
        g = g_next
        if k + 1 < len(subs):
            g_next = glu_logits(subs[k + 1])
        z2 = z_rows(rows).astype(f32) * (1.0 / (1.0 + jnp.exp(-g)))
        m = _dot(z2.astype(bf16), wo_ref[...])
        h_ref[rows, :] = x_ref[rows, :] + gate * m


def _s5_out(z, x, mod4, l, wg_bf, wo_bf, tm):
    B, S, D = x.shape
    row = pl.BlockSpec((None, tm, D), lambda b, i: (b, i, 0))
    zblk = pl.BlockSpec((None, D // LANES, tm, LANES), lambda b, i: (b, 0, i, 0))
    return pl.pallas_call(
        _s5_out_kernel,
        grid=(B, S // tm),
        in_specs=[zblk, row, _mod_spec(D)(l), _resident((D, D)), _resident((D, D))],
        out_specs=row,
        out_shape=jax.ShapeDtypeStruct((B, S, D), f32),
        compiler_params=_cparams("parallel", "parallel"),
        name="s5_out",
    )(z, x, mod4, wg_bf, wo_bf)


def _ffn_kernel(h_ref, halo_ref, mod_ref, g_ref, wu_ref, wd_ref, cw_ref, cb_ref,
                fg_ref, o_ref, hn_ref, a_ref, b_ref, gt_ref, *, fc, final_norm):
    i = pl.program_id(1)
    tm = h_ref.shape[0]
    hb = halo_ref.shape[0]
    taps, F = cw_ref.shape
    nc = F // fc
    g, shift, scale = g_ref[...], mod_ref[0:1, :], mod_ref[1:2, :]
    x = h_ref[...]
    keep = (i > 0).astype(f32)
    hn_ref[0:hb, :] = (_modulate(halo_ref[...], g, shift, scale) * keep).astype(bf16)
    a_ref[0, 0:hb, :] = _dot(hn_ref[0:hb, :], wu_ref[:, 0:fc])
    for rows in _subtiles(tm):
        hrows = slice(hb + rows.start, hb + rows.stop)
        hn = _modulate(h_ref[rows, :], g, shift, scale).astype(bf16)
        hn_ref[hrows, :] = hn
        a_ref[0, hrows, :] = _dot(hn, wu_ref[:, 0:fc])
        b_ref[0, rows, :] = _dot(hn, wu_ref[:, F:F + fc])

    def up(j, slot):
        cols = slice(j * fc, (j + 1) * fc)
        a_ref[slot] = _dot(hn_ref[...], wu_ref[:, cols])
        b_ref[slot] = _dot(hn_ref[hb:, :], wu_ref[:, F + j * fc:F + (j + 1) * fc])

    def gate(j, slot):
        cols = slice(j * fc, (j + 1) * fc)
        cw = cw_ref[:, cols]
        ac = cb_ref[:, cols]
        for t in range(taps):
            off = hb - (taps - 1) + t
            ac = ac + cw[t:t + 1, :] * a_ref[slot, off:off + tm, :]
        gt_ref[:, cols] = (_silu(ac) * b_ref[slot]).astype(bf16)

    for j in range(nc):
        if j + 1 < nc:
            up(j + 1, (j + 1) % 2)
        gate(j, j % 2)
    out = x + mod_ref[2:3, :] * _dot(gt_ref[...], wd_ref[...])
    if final_norm:
        ms = jnp.mean(out * out, axis=-1, keepdims=True)
        out = out * lax.rsqrt(ms + EPS) * fg_ref[...]
    o_ref[...] = out


def _ffn(h, mod4, l, g, w_up, conv_w, conv_b, w_down, final_g, tm, fc, final_norm):
    B, S, D = h.shape
    F = w_down.shape[0]
    taps = conv_w.shape[0]
    assert F % fc == 0 and F % LANES == 0
    hb = BF16_ROWS
    per = tm // hb
    kern = functools.partial(_ffn_kernel, fc=fc, final_norm=final_norm)
    return pl.pallas_call(
        kern,
        grid=(B, S // tm),
        in_specs=[
            pl.BlockSpec((None, tm, D), lambda b, i: (b, i, 0)),
            pl.BlockSpec((None, hb, D), lambda b, i: (b, jnp.maximum(i * per - 1, 0), 0)),
            _mod_spec(D)(l),
            _resident((1, D)),
            _resident((D, 2 * F)),
            _resident((F, D)),
            _resident((taps, F)),
            _resident((1, F)),
            _resident((1, D)),
        ],
        out_specs=pl.BlockSpec((None, tm, D), lambda b, i: (b, i, 0)),
        out_shape=jax.ShapeDtypeStruct((B, S, D), f32),
        scratch_shapes=[
            pltpu.VMEM((hb + tm, D), bf16),
            pltpu.VMEM((2, hb + tm, fc), f32),
            pltpu.VMEM((2, tm, fc), f32),
            pltpu.VMEM((tm, F), bf16),
        ],
        compiler_params=_cparams("parallel", "parallel"),
        name="ffn",
    )(h, h, mod4, g, w_up.astype(bf16), w_down.astype(bf16), conv_w, conv_b.reshape(1, F),
      final_g)


LOG2E = 1.4426950408889634


def _fox_in_kernel(h_ref, mod_ref, g_ref, wqt_ref, wk_ref, wvt_ref, wf_ref, bf_ref,
                   qt_ref, k_ref, vt_ref, fc_ref, carry_ref, *, q_scale, ts):
    i = pl.program_id(1)
    tm, D = h_ref.shape

    @pl.when(i == 0)
    def _():
        carry_ref[...] = jnp.zeros_like(carry_ref)

    g, shift, scale = g_ref[...], mod_ref[0:1, :], mod_ref[1:2, :]
    subs = _subtiles(tm)
    sub = subs[0].stop
    r = lax.broadcasted_iota(jnp.int32, (sub, sub), 0)
    c = lax.broadcasted_iota(jnp.int32, (sub, sub), 1)
    tri = (c <= r).astype(bf16)
    carry = carry_ref[...]
    for rows in subs:
        hn = _modulate(h_ref[rows, :], g, shift, scale).astype(bf16)
        k_ref[rows, :] = _dot(hn, wk_ref[...]).astype(bf16)
        qt_ref[:, rows] = (_dot_nt(wqt_ref[...], hn) * q_scale).astype(bf16)
        vt = _dot_nt(wvt_ref[...], hn).astype(bf16)
        for cc in range(sub // ts):
            vt_ref[rows.start // ts + cc] = vt[:, cc * ts:(cc + 1) * ts]

        fl = _dot(hn, wf_ref[...]) + bf_ref[...]
        lf = jnp.minimum(fl, 0.0) - jnp.log(1.0 + jnp.exp(-jnp.abs(fl)))
        p0, p1, p2 = _split3(lf)
        cs = _dot(tri, p0) + _dot(tri, p1) + _dot(tri, p2) + carry
        fc_ref[rows, :] = cs * LOG2E
        carry = cs[sub - 1:sub, :]
    carry_ref[...] = carry


def _fox_in(h, mod4, l, g, wqt_bf, wk_bf, wvt_bf, wf_bf, bf_pad, q_scale, tm, ts):
    B, S, D = h.shape
    return pl.pallas_call(
        functools.partial(_fox_in_kernel, q_scale=q_scale, ts=ts),
        grid=(B, S // tm),
        in_specs=[
            pl.BlockSpec((None, tm, D), lambda b, i: (b, i, 0)),
            _mod_spec(D)(l),
            _resident((1, D)),
            _resident((D, D)),
            _resident((D, D)),
            _resident((D, D)),
            _resident((D, LANES)),
            _resident((1, LANES)),
        ],
        out_specs=[
            pl.BlockSpec((None, D, tm), lambda b, i: (b, 0, i)),
            pl.BlockSpec((None, tm, D), lambda b, i: (b, i, 0)),
            pl.BlockSpec((None, tm // ts, D, ts), lambda b, i: (b, i, 0, 0)),
            pl.BlockSpec((None, tm, LANES), lambda b, i: (b, i, 0)),
        ],
        out_shape=[
            jax.ShapeDtypeStruct((B, D, S), bf16),
            jax.ShapeDtypeStruct((B, S, D), bf16),
            jax.ShapeDtypeStruct((B, S // ts, D, ts), bf16),
            jax.ShapeDtypeStruct((B, S, LANES), f32),
        ],
        scratch_shapes=[pltpu.VMEM((1, LANES), f32)],
        compiler_params=_cparams("parallel", "arbitrary"),
        name="fox_in",
    )(h, mod4, g, wqt_bf, wk_bf, wvt_bf, wf_bf, bf_pad)


NEG_BIG = -1e30
ATTN_PAIR = 2
ATTN_LOOKAHEAD = 2


def _sublane_allreduce(x, op):
    for sh in (4, 2, 1):
        x = op(x, pltpu.roll(x, sh, axis=0))
    return x


def _tree(xs, op):
    while len(xs) > 1:
        xs = [op(xs[i], xs[i + 1]) if i + 1 < len(xs) else xs[i] for i in range(0, len(xs), 2)]
    return xs[0]


def _attn_kernel(qt_ref, k_ref, vt_ref, fk_ref, o_ref, fkb_ref, s_ref, mx_ref, *, ts, dh):
    qi = pl.program_id(2)
    tq = qt_ref.shape[1]
    heads = LANES // dh
    spq = tq // ts

    @pl.when(qi == 0)
    def _():
        def fill(a, carry):
            for e in range(heads):
                row = fk_ref[e, pl.ds(a, 1), :]
                rows = pl.ds(pl.multiple_of(a * ts, ts), ts)
                fkb_ref[e, rows, :] = jnp.broadcast_to(row, (LANES, ts)).T
            return carry

        lax.fori_loop(0, k_ref.shape[0] // ts, fill, 0)

    qt = qt_ref[...]
    rowi = lax.broadcasted_iota(jnp.int32, qt.shape, 0)
    qts = [jnp.where((rowi // dh) == e, qt, jnp.zeros_like(qt)) for e in range(heads)]
    nrt = ts // SUBLANES

    def issue(j, lane0=0):
        jd, slot = j
        ks = pl.ds(pl.multiple_of(jd * ts, ts), ts)
        kb = k_ref[ks, :]
        for e in range(heads):
            fkb = fkb_ref[e, ks, :]
            s = _dot(kb, qts[e][:, lane0:]) - jnp.concatenate([fkb] * ((tq - lane0) // LANES),
                                                              axis=1)
            s_ref[slot, e, :, lane0:] = s
            if lane0 == 0:
                mx_ref[slot, e] = _tree([s[r * SUBLANES:(r + 1) * SUBLANES, :]
                                         for r in range(nrt)], jnp.maximum)

    ones_rows = (lax.broadcasted_iota(jnp.int32, (BF16_ROWS, ATTN_PAIR * ts), 0) == 0).astype(bf16)
    acc_rows = dh + SUBLANES

    def absorb(js, carry, key_offs):
        vtb = jnp.concatenate([vt_ref[jd] for jd, _ in js], axis=1)
        lane0 = 0 if key_offs is None else min(key_offs)
        out = []
        for e in range(heads):
            m_all, acc_all = carry[2 * e:2 * e + 2]
            m, acc = m_all[:, lane0:], acc_all[:, lane0:]
            if key_offs is None:
                tiles = None
                mx = _tree([mx_ref[slot, e] for _, slot in js], jnp.maximum)
            else:
                tiles = []
                for (_, slot), off in zip(js, key_offs):
                    kidx = lax.broadcasted_iota(jnp.int32, (ts, tq - lane0), 0) + off
                    qidx = lax.broadcasted_iota(jnp.int32, (ts, tq - lane0), 1) + lane0
                    sm = jnp.where(kidx <= qidx, s_ref[slot, e, :, lane0:], NEG_BIG)
                    tiles += [sm[r * SUBLANES:(r + 1) * SUBLANES, :] for r in range(nrt)]
                mx = _tree(tiles, jnp.maximum)
            m_new = jnp.maximum(m, _sublane_allreduce(mx, jnp.maximum))
            alpha = jnp.exp2(m - m_new)
            if tiles is None:
                tiles = [s_ref[slot, e, r * SUBLANES:(r + 1) * SUBLANES, :]
                         for _, slot in js for r in range(nrt)]
            p = jnp.concatenate([jnp.exp2(t - m_new) for t in tiles], axis=0).astype(bf16)
            lhs = jnp.concatenate([vtb[e * dh:(e + 1) * dh, :], ones_rows], axis=0)
            pv = _dot(lhs, p)[:acc_rows, :]
            acc = jnp.concatenate([alpha] * (acc_rows // SUBLANES), axis=0) * acc + pv
            if lane0:
                m_new = jnp.concatenate([m_all[:, :lane0], m_new], axis=1)
                acc = jnp.concatenate([acc_all[:, :lane0], acc], axis=1)
            out += [m_new, acc]
        return tuple(out)

    la = ATTN_LOOKAHEAD
    assert la + ATTN_PAIR <= spq and spq % ATTN_PAIR == 0

    def strip_id(j0, d):
        return (j0 + d, d % spq)

    def group(j0, carry, masked):
        last = None if not masked else spq
        for d0 in range(0, spq, ATTN_PAIR):
            ds = range(d0, d0 + ATTN_PAIR)
            for d in ds:
                if last is None:
                    issue(strip_id(j0, d + la))
                elif d + la < last:
                    issue(strip_id(j0, d + la), (d + la) // ATTN_PAIR * ATTN_PAIR * ts)
            carry = absorb([strip_id(j0, d) for d in ds], carry,
                           [d * ts for d in ds] if masked else None)
        return carry

    for d in range(la):
        issue(strip_id(0, d))
    init = []
    for e in range(heads):
        init += [jnp.full((SUBLANES, tq), NEG_BIG, f32), jnp.zeros((acc_rows, tq), f32)]
    def pair(t, c):
        return group((2 * t + 1) * spq, group(2 * t * spq, c, False), False)

    carry = lax.fori_loop(0, qi // 2, pair, tuple(init))
    carry = lax.fori_loop(qi - qi % 2, qi, lambda g, c: group(g * spq, c, False), carry)
    carry = group(qi * spq, carry, True)
    outs = []
    for e in range(heads):
        acc = carry[2 * e + 1]
        l = jnp.broadcast_to(acc[dh:dh + 1, :], (dh, tq))
        outs.append(acc[:dh, :] / l)
    o_ref[...] = jnp.concatenate(outs, axis=0).T.astype(o_ref.dtype)


def _attn(qt, k, vt, fk, n_heads, tq):
    B, D, S = qt.shape
    ts = vt.shape[-1]
    dh = D // n_heads
    heads = LANES // dh
    nhp = D // LANES
    kern = functools.partial(_attn_kernel, ts=ts, dh=dh)
    return pl.pallas_call(
        kern,
        grid=(B, nhp, S // tq),
        in_specs=[
            pl.BlockSpec((None, LANES, tq), lambda b, h, i: (b, h, i)),
            pl.BlockSpec((None, S, LANES), lambda b, h, i: (b, 0, h)),
            pl.BlockSpec((None, S // ts, LANES, ts), lambda b, h, i: (b, 0, h, 0)),
            pl.BlockSpec((None, heads, S // ts, ts), lambda b, h, i: (b, h, 0, 0)),
        ],
        out_specs=pl.BlockSpec((None, tq, LANES), lambda b, h, i: (b, i, h)),
        out_shape=jax.ShapeDtypeStruct((B, S, D), bf16),
        scratch_shapes=[pltpu.VMEM((heads, S, LANES), f32),
                        pltpu.VMEM((tq // ts, heads, ts, tq), f32),
                        pltpu.VMEM((tq // ts, heads, SUBLANES, tq), f32)],
        compiler_params=_cparams("parallel", "parallel", "arbitrary"),
        name="attn",
    )(qt, k, vt, fk)


def _fox_out_kernel(o_ref, h_ref, mod_ref, w_ref, out_ref):
    gate = mod_ref[2:3, :]
    for rows in _subtiles(o_ref.shape[0]):
        out_ref[rows, :] = h_ref[rows, :] + gate * _dot(o_ref[rows, :], w_ref[...])


def _fox_out(o, h, mod4, l, w_bf, tm):
    B, S, D = h.shape
    row = pl.BlockSpec((None, tm, D), lambda b, i: (b, i, 0))
    return pl.pallas_call(
        _fox_out_kernel,
        grid=(B, S // tm),
        in_specs=[row, row, _mod_spec(D)(l), _resident((D, D))],
        out_specs=row,
        out_shape=jax.ShapeDtypeStruct((B, S, D), f32),
        compiler_params=_cparams("parallel", "parallel"),
        name="fox_out",
    )(o, h, mod4, w_bf)


def kernel(x, c, norm_g, ada_w, ada_b, s5_w_in, s5_lam_re, s5_lam_im, s5_log_dt, s5_b_re, s5_b_im, s5_c_re, s5_c_im, s5_d, s5_w_glu, s5_w_out, fox_w_in, fox_b_f, fox_w_out, ffn_w_up, ffn_conv_w, ffn_conv_b, ffn_w_down, final_g):
    B, S, D = x.shape
    depth = norm_g.shape[0]
    n_heads = fox_b_f.shape[-1]
    tm = min(512, S)
    tmm = min(1024, S)
    fc = 256
    final = final_g.reshape(1, D)

    mod = _adaln(c, ada_w, ada_b)
    mod4 = mod.reshape(depth * 2, B, 3, D)

    h = x
    for i in range(depth):
        j = i // 2
        g_mix = norm_g[i, 0].reshape(1, D)
        g_ffn = norm_g[i, 1].reshape(1, D)
        if i % 2 == 0:
            u = _s5_in(h, mod4, 2 * i, g_mix, s5_w_in[j].astype(bf16), tmm)
            wcs, wout, wtz, adec = _s5_weights(s5_lam_re[j], s5_lam_im[j], s5_log_dt[j],
                                               s5_b_re[j], s5_b_im[j], s5_c_re[j], s5_c_im[j],
                                               S5_CHUNK)
            z = _s5_core(u, wcs, wout, wtz, adec, s5_d[j].reshape(1, D), S5_CHUNK)
            h = _s5_out(z, h, mod4, 2 * i, s5_w_glu[j].astype(bf16),
                        s5_w_out[j].astype(bf16), tmm)
        else:
            w_in = fox_w_in[j]
            wqt = w_in[:, :D].T.astype(bf16)
            wk = w_in[:, D:2 * D].astype(bf16)
            wvt = w_in[:, 2 * D:3 * D].T.astype(bf16)
            wf = jnp.pad(w_in[:, 3 * D:], ((0, 0), (0, LANES - n_heads))).astype(bf16)
            bfp = jnp.pad(fox_b_f[j], (0, LANES - n_heads)).reshape(1, LANES)
            ts = LANES
            q_scale = float((D // n_heads) ** -0.5 * LOG2E)
            qt, k, vt, fcum = _fox_in(h, mod4, 2 * i, g_mix, wqt, wk, wvt, wf, bfp,
                                      q_scale, tmm, ts)
            fk = fcum[:, :, :n_heads].transpose(0, 2, 1).reshape(B, n_heads, S // ts, ts)
            o = _attn(qt, k, vt, fk, n_heads, tq=min(512, S))
            h = _fox_out(o, h, mod4, 2 * i, fox_w_out[j].astype(bf16), tmm)
        h = _ffn(h, mod4, 2 * i + 1, g_ffn, ffn_w_up[i], ffn_conv_w[i], ffn_conv_b[i],
                 ffn_w_down[i], final, tm, fc, final_norm=(i == depth - 1))
    return h
```

```python
import functools

import jax
import jax.numpy as jnp
import numpy as np
from jax import lax
from jax.experimental import pallas as pl
from jax.experimental.pallas import tpu as pltpu

EPS = 1e-6
LANES = 128
SUBLANES = 8
BF16_ROWS = 16
VMEM_LIMIT = 56 * 1024 * 1024

S5_CHUNK = 4
S5_SEGMENTS = SUBLANES
S5_SCAN_UNROLL = 8

f32 = jnp.float32
bf16 = jnp.bfloat16


def _cparams(*sem):
    return pltpu.CompilerParams(dimension_semantics=sem, vmem_limit_bytes=VMEM_LIMIT)


def _resident(shape):
    nd = len(shape)
    return pl.BlockSpec(shape, lambda *_: (0,) * nd, pipeline_mode=pl.Buffered(1))


def _dot(a, b):
    return jnp.dot(a, b, preferred_element_type=f32)


def _dot_nt(a, b):
    return lax.dot_general(a, b, (((1,), (1,)), ((), ())), preferred_element_type=f32)


SUB_ROWS = 256


def _subtiles(tm):
    sub = min(SUB_ROWS, tm)
    assert tm % sub == 0
    return [slice(k * sub, (k + 1) * sub) for k in range(tm // sub)]


def _split3(x):
    hi = x.astype(bf16)
    r1 = x - hi.astype(f32)
    mid = r1.astype(bf16)
    lo = (r1 - mid.astype(f32)).astype(bf16)
    return hi, mid, lo


def _modulate(x, g, shift, scale):
    ms = jnp.mean(x * x, axis=-1, keepdims=True)
    return x * lax.rsqrt(ms + EPS) * g * (1.0 + scale) + shift


def _silu(x):
    return x * (1.0 / (1.0 + jnp.exp(-x)))


def _gelu_tanh(x):
    c = 0.7978845608028654
    return 0.5 * x * (1.0 + jnp.tanh(c * (x + 0.044715 * (x * x * x))))


def _adaln_kernel(ct_ref, w_ref, b_ref, o_ref):
    ct = ct_ref[...]
    cs = _silu(ct)
    w = w_ref[...]
    for b in range(ct.shape[1]):
        col = cs[:, b:b + 1]
        o_ref[b:b + 1, :] = jnp.sum(w * col, axis=0, keepdims=True) + b_ref[...]


def _adaln(c, ada_w, ada_b):
    B, D = c.shape
    L = ada_w.shape[0] * ada_w.shape[1]
    N = ada_w.shape[-1]
    tn = 512
    w = ada_w.reshape(L, D, N)
    bias = ada_b.reshape(L, 1, N)
    return pl.pallas_call(
        _adaln_kernel,
        grid=(L, N // tn),
        in_specs=[
            pl.BlockSpec((D, B), lambda l, j: (0, 0)),
            pl.BlockSpec((None, D, tn), lambda l, j: (l, 0, j)),
            pl.BlockSpec((None, 1, tn), lambda l, j: (l, 0, j)),
        ],
        out_specs=pl.BlockSpec((None, B, tn), lambda l, j: (l, 0, j)),
        out_shape=jax.ShapeDtypeStruct((L, B, N), f32),
        compiler_params=_cparams("parallel", "parallel"),
        name="adaln",
    )(c.T, w, bias)


def _mod_spec(D):
    return lambda l: pl.BlockSpec((None, None, 3, D), lambda b, i: (l, b, 0, 0))


def _s5_in_kernel(x_ref, mod_ref, g_ref, w_ref, u_ref):
    g, shift, scale = g_ref[...], mod_ref[0:1, :], mod_ref[1:2, :]
    for rows in _subtiles(x_ref.shape[0]):
        hn = _modulate(x_ref[rows, :], g, shift, scale)
        u = _dot(hn.astype(bf16), w_ref[...])
        for k in range(u_ref.shape[0]):
            u_ref[k, rows, :] = u[:, k * LANES:(k + 1) * LANES]


def _s5_in(x, mod4, l, g, w_bf, tm):
    B, S, D = x.shape
    return pl.pallas_call(
        _s5_in_kernel,
        grid=(B, S // tm),
        in_specs=[
            pl.BlockSpec((None, tm, D), lambda b, i: (b, i, 0)),
            _mod_spec(D)(l),
            _resident((1, D)),
            _resident((D, D)),
        ],
        out_specs=pl.BlockSpec((None, D // LANES, tm, LANES), lambda b, i: (b, 0, i, 0)),
        out_shape=jax.ShapeDtypeStruct((B, D // LANES, S, LANES), f32),
        compiler_params=_cparams("parallel", "parallel"),
        name="s5_in",
    )(x, mod4, g, w_bf)


def _s5_weights_kernel(lre_ref, lim_ref, ldt_ref, br_ref, bi_ref, cr_ref, ci_ref,
                       wcs_ref, wout_ref, wtz_ref, adec_ref, *, lc, cg, p_state):
    lre = lre_ref[...]
    lim = lim_ref[...]
    dt = jnp.exp(ldt_ref[...])
    nl = lre.shape[1]
    mag = jnp.exp(lre * dt)
    lam_r = mag * jnp.cos(lim * dt)
    lam_i = mag * jnp.sin(lim * dt)
    num_r = lam_r - 1.0
    den = lre * lre + lim * lim
    k_r = (num_r * lre + lam_i * lim) / den
    k_i = (lam_i * lre - num_r * lim) / den
    br = br_ref[...]
    bi = bi_ref[...]
    bb_r = k_r * br - k_i * bi
    bb_i = k_r * bi + k_i * br
    cr = cr_ref[...]
    ci = ci_ref[...]

    ngrp = nl // p_state
    rows = ngrp * cg
    row_grp = lax.broadcasted_iota(jnp.int32, (rows, nl), 0) // cg
    lane_grp = lax.broadcasted_iota(jnp.int32, (rows, nl), 1) // p_state
    own = row_grp == lane_grp

    def slab(xr, xi):
        tr = jnp.where(own, jnp.tile(xr, (ngrp, 1)), 0.0)
        ti = jnp.where(own, jnp.tile(xi, (ngrp, 1)), 0.0)
        return jnp.concatenate([tr, ti], axis=1)

    pw = [(jnp.ones_like(lam_r), jnp.zeros_like(lam_r))]
    for _ in range(lc):
        pr, pi = pw[-1]
        pw.append((pr * lam_r - pi * lam_i, pr * lam_i + pi * lam_r))
    adec_ref[0:1, :] = pw[lc][0]
    adec_ref[1:2, :] = pw[lc][1]

    for s in range(lc):
        pr, pi = pw[lc - 1 - s]
        wcs_ref[s * rows:(s + 1) * rows, :] = slab(pr * bb_r - pi * bb_i,
                                                   pr * bb_i + pi * bb_r).astype(bf16)

    wq = []
    for t in range(lc + 1):
        pr, pi = pw[t]
        wq.append(slab(cr * pr - ci * pi, -(cr * pi + ci * pr)))
    for r in range(lc):
        wout_ref[r * rows:(r + 1) * rows, :] = wq[r + 1].astype(bf16)

    bsl = _split3(slab(bb_r, bb_i))
    kt = []
    for t in range(lc):
        q = _split3(wq[t])
        acc = _dot_nt(bsl[0], q[0])
        acc += _dot_nt(bsl[0], q[1]) + _dot_nt(bsl[1], q[0])
        acc += _dot_nt(bsl[1], q[1]) + _dot_nt(bsl[0], q[2]) + _dot_nt(bsl[2], q[0])
        kt.append(acc)
    zero = jnp.zeros((rows, rows), f32)
    for s in range(lc):
        for r in range(lc):
            blk = kt[r - s] if r >= s else zero
            wtz_ref[s * rows:(s + 1) * rows, r * rows:(r + 1) * rows] = blk.astype(bf16)


def _s5_weights(lam_re, lam_im, log_dt, b_re, b_im, c_re, c_im, lc):
    G, P = lam_re.shape
    Cg = b_re.shape[-1]
    gb = LANES // Cg
    nb = G // gb
    nl = gb * P
    rows = gb * Cg
    lre = lam_re.reshape(nb, 1, nl)
    lim = lam_im.reshape(nb, 1, nl)
    ldt = jnp.repeat(log_dt, P).reshape(nb, 1, nl)
    brt = b_re.reshape(nb, gb, P, Cg).transpose(0, 3, 1, 2).reshape(nb, Cg, nl)
    bit = b_im.reshape(nb, gb, P, Cg).transpose(0, 3, 1, 2).reshape(nb, Cg, nl)
    crt = c_re.reshape(nb, gb, Cg, P).transpose(0, 2, 1, 3).reshape(nb, Cg, nl)
    cit = c_im.reshape(nb, gb, Cg, P).transpose(0, 2, 1, 3).reshape(nb, Cg, nl)
    vec = pl.BlockSpec((None, 1, nl), lambda k: (k, 0, 0))
    mat = pl.BlockSpec((None, Cg, nl), lambda k: (k, 0, 0))
    return pl.pallas_call(
        functools.partial(_s5_weights_kernel, lc=lc, cg=Cg, p_state=P),
        grid=(nb,),
        in_specs=[vec, vec, vec, mat, mat, mat, mat],
        out_specs=[
            pl.BlockSpec((None, lc * rows, 2 * nl), lambda k: (k, 0, 0)),
            pl.BlockSpec((None, lc * rows, 2 * nl), lambda k: (k, 0, 0)),
            pl.BlockSpec((None, lc * rows, lc * rows), lambda k: (k, 0, 0)),
            pl.BlockSpec((None, 2, nl), lambda k: (k, 0, 0)),
        ],
        out_shape=[
            jax.ShapeDtypeStruct((nb, lc * rows, 2 * nl), bf16),
            jax.ShapeDtypeStruct((nb, lc * rows, 2 * nl), bf16),
            jax.ShapeDtypeStruct((nb, lc * rows, lc * rows), bf16),
            jax.ShapeDtypeStruct((nb, 2, nl), f32),
        ],
        compiler_params=_cparams("parallel"),
        name="s5_weights",
    )(lre, lim, ldt, brt, bit, crt, cit)


def _s5_core_kernel(u_ref, wcs_ref, wout_ref, wtz_ref, adec_ref, d_ref, z_ref,
                    st_ref, y1_ref, y_ref, *, lc, nseg, seg_len):
    nslab = st_ref.shape[0]
    half = nslab // 2
    seg_tok = seg_len * lc

    def phase_a(seg, carry):
        tok0 = seg * seg_tok
        x = jnp.concatenate(
            [u_ref[pl.ds(tok0 + s, seg_len, stride=lc), :] for s in range(lc)], axis=1)
        xb = x.astype(bf16)
        e = _dot(xb, wcs_ref[...])
        for j in range(nslab):
            st_ref[j, pl.ds(seg, seg_len, stride=nseg), :] = e[:, j * LANES:(j + 1) * LANES]
        y1_ref[pl.ds(seg * seg_len, seg_len), :] = _dot(xb, wtz_ref[...])
        return carry

    for seg in range(nseg):
        phase_a(seg, 0)

    a_r = [jnp.broadcast_to(adec_ref[0:1, j * LANES:(j + 1) * LANES], (nseg, LANES))
           for j in range(half)]
    a_i = [jnp.broadcast_to(adec_ref[1:2, j * LANES:(j + 1) * LANES], (nseg, LANES))
           for j in range(half)]

    def step(i, hs, store):
        out = []
        rows = pl.ds(pl.multiple_of(i * nseg, nseg), nseg)
        for j in range(half):
            hr, hi = hs[2 * j], hs[2 * j + 1]
            er = st_ref[j, rows, :]
            ei = st_ref[half + j, rows, :]
            if store:
                st_ref[j, rows, :] = hr
                st_ref[half + j, rows, :] = hi
            out.append(a_r[j] * hr - a_i[j] * hi + er)
            out.append(a_r[j] * hi + a_i[j] * hr + ei)
        return tuple(out)

    zeros = tuple(jnp.zeros((nseg, LANES), f32) for _ in range(nslab))
    ends = lax.fori_loop(0, seg_len, lambda i, hs: step(i, hs, False), zeros,
                         unroll=S5_SCAN_UNROLL)

    starts = []
    for j in range(half):
        pr, pi = a_r[j][0:1, :], a_i[j][0:1, :]
        n = seg_len
        while n > 1:
            pr, pi = pr * pr - pi * pi, 2.0 * pr * pi
            n //= 2
        er, ei = ends[2 * j], ends[2 * j + 1]
        sr = [jnp.zeros((1, LANES), f32)]
        si = [jnp.zeros((1, LANES), f32)]
        for k in range(1, nseg):
            sr.append(pr * sr[-1] - pi * si[-1] + er[k - 1:k, :])
            si.append(pr * si[k - 1] + pi * sr[k - 1] + ei[k - 1:k, :])
        starts.append(jnp.concatenate(sr, axis=0))
        starts.append(jnp.concatenate(si, axis=0))
    lax.fori_loop(0, seg_len, lambda i, hs: step(i, hs, True), tuple(starts),
                  unroll=S5_SCAN_UNROLL)

    d = d_ref[...]

    def phase_c(seg):
        h = jnp.concatenate([st_ref[j, pl.ds(seg, seg_len, stride=nseg), :]
                             for j in range(nslab)], axis=1)
        y = _dot_nt(h.astype(bf16), wout_ref[...])
        y = y + y1_ref[pl.ds(seg * seg_len, seg_len), :]
        tok0 = seg * seg_tok
        for r in range(lc):
            y_ref[pl.ds(tok0 + r, seg_len, stride=lc), :] = y[:, r * LANES:(r + 1) * LANES]

    def phase_d(seg):
        rows = pl.ds(seg * seg_tok, seg_tok)
        z_ref[rows, :] = _gelu_tanh(y_ref[rows, :] + d * u_ref[rows, :]).astype(z_ref.dtype)

    phase_c(0)
    for seg in range(nseg):
        if seg + 1 < nseg:
            phase_c(seg + 1)
        phase_d(seg)


def _s5_core(u, wcs, wout, wtz, adec, d_skip, lc):
    B, nb, S, _ = u.shape
    nseg = S5_SEGMENTS
    m = S // lc
    seg_len = m // nseg
    assert seg_len * nseg * lc == S and seg_len & (seg_len - 1) == 0
    nslab = wcs.shape[-1] // LANES
    kw = wcs.shape[1]
    kern = functools.partial(_s5_core_kernel, lc=lc, nseg=nseg, seg_len=seg_len)
    return pl.pallas_call(
        kern,
        grid=(nb, B),
        in_specs=[
            pl.BlockSpec((None, None, S, LANES), lambda k, b: (b, k, 0, 0)),
            pl.BlockSpec((None, kw, nslab * LANES), lambda k, b: (k, 0, 0)),
            pl.BlockSpec((None, kw, nslab * LANES), lambda k, b: (k, 0, 0)),
            pl.BlockSpec((None, kw, kw), lambda k, b: (k, 0, 0)),
            pl.BlockSpec((None, 2, adec.shape[-1]), lambda k, b: (k, 0, 0)),
            pl.BlockSpec((1, LANES), lambda k, b: (0, k)),
        ],
        out_specs=pl.BlockSpec((None, None, S, LANES), lambda k, b: (b, k, 0, 0)),
        out_shape=jax.ShapeDtypeStruct((B, nb, S, LANES), bf16),
        scratch_shapes=[
            pltpu.VMEM((nslab, nseg * seg_len, LANES), f32),
            pltpu.VMEM((m, kw), f32),
            pltpu.VMEM((S, LANES), f32),
        ],
        compiler_params=_cparams("parallel", "parallel"),
        name="s5_core",
    )(u, wcs, wout, wtz, adec, d_skip)


def _s5_out_kernel(z_ref, x_ref, mod_ref, wg_ref, wo_ref, h_ref):
    subs = _subtiles(x_ref.shape[0])
    gate = mod_ref[2:3, :]

    def z_rows(rows):
        return jnp.concatenate([z_ref[k, rows, :] for k in range(z_ref.shape[0])], axis=1)

    def glu_logits(rows):
        return _dot(z_rows(rows), wg_ref[...])

    g_next = glu_logits(subs[0])
    for k, rows in enumerate(subs):
        g = g_next
        if k + 1 < len(subs):
            g_next = glu_logits(subs[k + 1])
        z2 = z_rows(rows).astype(f32) * (1.0 / (1.0 + jnp.exp(-g)))
        m = _dot(z2.astype(bf16), wo_ref[...])
        h_ref[rows, :] = x_ref[rows, :] + gate * m


def _s5_out(z, x, mod4, l, wg_bf, wo_bf, tm):
    B, S, D = x.shape
    row = pl.BlockSpec((None, tm, D), lambda b, i: (b, i, 0))
    zblk = pl.BlockSpec((None, D // LANES, tm, LANES), lambda b, i: (b, 0, i, 0))
    return pl.pallas_call(
        _s5_out_kernel,
        grid=(B, S // tm),
        in_specs=[zblk, row, _mod_spec(D)(l), _resident((D, D)), _resident((D, D))],
        out_specs=row,
        out_shape=jax.ShapeDtypeStruct((B, S, D), f32),
        compiler_params=_cparams("parallel", "parallel"),
        name="s5_out",
    )(z, x, mod4, wg_bf, wo_bf)


def _ffn_kernel(h_ref, halo_ref, mod_ref, g_ref, wu_ref, wd_ref, cw_ref, cb_ref,
                fg_ref, o_ref, hn_ref, a_ref, b_ref, gt_ref, *, fc, final_norm):
    i = pl.program_id(1)
    tm = h_ref.shape[0]
    hb = halo_ref.shape[0]
    taps, F = cw_ref.shape
    nc = F // fc
    g, shift, scale = g_ref[...], mod_ref[0:1, :], mod_ref[1:2, :]
    x = h_ref[...]
    keep = (i > 0).astype(f32)
    hn_ref[0:hb, :] = (_modulate(halo_ref[...], g, shift, scale) * keep).astype(bf16)
    a_ref[0, 0:hb, :] = _dot(hn_ref[0:hb, :], wu_ref[:, 0:fc])
    for rows in _subtiles(tm):
        hrows = slice(hb + rows.start, hb + rows.stop)
        hn = _modulate(h_ref[rows, :], g, shift, scale).astype(bf16)
        hn_ref[hrows, :] = hn
        a_ref[0, hrows, :] = _dot(hn, wu_ref[:, 0:fc])
        b_ref[0, rows, :] = _dot(hn, wu_ref[:, F:F + fc])

    def up(j, slot):
        cols = slice(j * fc, (j + 1) * fc)
        a_ref[slot] = _dot(hn_ref[...], wu_ref[:, cols])
        b_ref[slot] = _dot(hn_ref[hb:, :], wu_ref[:, F + j * fc:F + (j + 1) * fc])

    def gate(j, slot):
        cols = slice(j * fc, (j + 1) * fc)
        cw = cw_ref[:, cols]
        ac = cb_ref[:, cols]
        for t in range(taps):
            off = hb - (taps - 1) + t
            ac = ac + cw[t:t + 1, :] * a_ref[slot, off:off + tm, :]
        gt_ref[:, cols] = (_silu(ac) * b_ref[slot]).astype(bf16)

    for j in range(nc):
        if j + 1 < nc:
            up(j + 1, (j + 1) % 2)
        gate(j, j % 2)
    out = x + mod_ref[2:3, :] * _dot(gt_ref[...], wd_ref[...])
    if final_norm:
        ms = jnp.mean(out * out, axis=-1, keepdims=True)
        out = out * lax.rsqrt(ms + EPS) * fg_ref[...]
    o_ref[...] = out


def _ffn(h, mod4, l, g, w_up, conv_w, conv_b, w_down, final_g, tm, fc, final_norm):
    B, S, D = h.shape
    F = w_down.shape[0]
    taps = conv_w.shape[0]
    assert F % fc == 0 and F % LANES == 0
    hb = BF16_ROWS
    per = tm // hb
    kern = functools.partial(_ffn_kernel, fc=fc, final_norm=final_norm)
    return pl.pallas_call(
        kern,
        grid=(B, S // tm),
        in_specs=[
            pl.BlockSpec((None, tm, D), lambda b, i: (b, i, 0)),
            pl.BlockSpec((None, hb, D), lambda b, i: (b, jnp.maximum(i * per - 1, 0), 0)),
            _mod_spec(D)(l),
            _resident((1, D)),
            _resident((D, 2 * F)),
            _resident((F, D)),
            _resident((taps, F)),
            _resident((1, F)),
            _resident((1, D)),
        ],
        out_specs=pl.BlockSpec((None, tm, D), lambda b, i: (b, i, 0)),
        out_shape=jax.ShapeDtypeStruct((B, S, D), f32),
        scratch_shapes=[
            pltpu.VMEM((hb + tm, D), bf16),
            pltpu.VMEM((2, hb + tm, fc), f32),
            pltpu.VMEM((2, tm, fc), f32),
            pltpu.VMEM((tm, F), bf16),
        ],
        compiler_params=_cparams("parallel", "parallel"),
        name="ffn",
    )(h, h, mod4, g, w_up.astype(bf16), w_down.astype(bf16), conv_w, conv_b.reshape(1, F),
      final_g)


LOG2E = 1.4426950408889634


FBIAS_PIECES = 3


def _fbias_selector(n_heads):
    heads = 2
    sel = np.zeros((FBIAS_PIECES * LANES, (n_heads // heads) * LANES), np.float32)
    for h in range(n_heads):
        for p in range(FBIAS_PIECES):
            sel[p * LANES + h, (h // heads) * LANES + (h % heads) * FBIAS_PIECES + p] = 1.0
    return jnp.asarray(sel, dtype=bf16)


def _fox_in_kernel(h_ref, mod_ref, g_ref, wqt_ref, wk_ref, wvt_ref, wf_ref, bf_ref, sel_ref,
                   qt_ref, k_ref, vt_ref, fa_ref, carry_ref, *, q_scale, ts):
    i = pl.program_id(1)
    tm, D = h_ref.shape

    @pl.when(i == 0)
    def _():
        carry_ref[...] = jnp.zeros_like(carry_ref)

    g, shift, scale = g_ref[...], mod_ref[0:1, :], mod_ref[1:2, :]
    subs = _subtiles(tm)
    sub = subs[0].stop
    r = lax.broadcasted_iota(jnp.int32, (sub, sub), 0)
    c = lax.broadcasted_iota(jnp.int32, (sub, sub), 1)
    tri = (c <= r).astype(bf16)
    carry = carry_ref[...]
    for rows in subs:
        hn = _modulate(h_ref[rows, :], g, shift, scale).astype(bf16)
        k_ref[rows, :] = _dot(hn, wk_ref[...]).astype(bf16)
        qt_ref[:, rows] = (_dot_nt(wqt_ref[...], hn) * q_scale).astype(bf16)
        vt = _dot_nt(wvt_ref[...], hn).astype(bf16)
        for cc in range(sub // ts):
            vt_ref[rows.start // ts + cc] = vt[:, cc * ts:(cc + 1) * ts]

        fl = _dot(hn, wf_ref[...]) + bf_ref[...]
        lf = jnp.minimum(fl, 0.0) - jnp.log(1.0 + jnp.exp(-jnp.abs(fl)))
        p0, p1, p2 = _split3(lf)
        cs = _dot(tri, p0) + _dot(tri, p1) + _dot(tri, p2) + carry
        carry = cs[sub - 1:sub, :]
        fa = _dot(jnp.concatenate(_split3(cs * LOG2E), axis=1), sel_ref[...]).astype(bf16)
        for hp in range(fa_ref.shape[0]):
            fa_ref[hp, rows, :] = fa[:, hp * LANES:(hp + 1) * LANES]
    carry_ref[...] = carry


def _fox_in(h, mod4, l, g, wqt_bf, wk_bf, wvt_bf, wf_bf, bf_pad, sel, q_scale, tm, ts):
    B, S, D = h.shape
    nhp = sel.shape[1] // LANES
    return pl.pallas_call(
        functools.partial(_fox_in_kernel, q_scale=q_scale, ts=ts),
        grid=(B, S // tm),
        in_specs=[
            pl.BlockSpec((None, tm, D), lambda b, i: (b, i, 0)),
            _mod_spec(D)(l),
            _resident((1, D)),
            _resident((D, D)),
            _resident((D, D)),
            _resident((D, D)),
            _resident((D, LANES)),
            _resident((1, LANES)),
            _resident(sel.shape),
        ],
        out_specs=[
            pl.BlockSpec((None, D, tm), lambda b, i: (b, 0, i)),
            pl.BlockSpec((None, tm, D), lambda b, i: (b, i, 0)),
            pl.BlockSpec((None, tm // ts, D, ts), lambda b, i: (b, i, 0, 0)),
            pl.BlockSpec((None, nhp, tm, LANES), lambda b, i: (b, 0, i, 0)),
        ],
        out_shape=[
            jax.ShapeDtypeStruct((B, D, S), bf16),
            jax.ShapeDtypeStruct((B, S, D), bf16),
            jax.ShapeDtypeStruct((B, S // ts, D, ts), bf16),
            jax.ShapeDtypeStruct((B, nhp, S, LANES), bf16),
        ],
        scratch_shapes=[pltpu.VMEM((1, LANES), f32)],
        compiler_params=_cparams("parallel", "arbitrary"),
        name="fox_in",
    )(h, mod4, g, wqt_bf, wk_bf, wvt_bf, wf_bf, bf_pad, sel)


NEG_BIG = -1e30
ATTN_PAIR = 2
ATTN_LOOKAHEAD = 2


def _sublane_allreduce(x, op):
    for sh in (4, 2, 1):
        x = op(x, pltpu.roll(x, sh, axis=0))
    return x


def _tree(xs, op):
    while len(xs) > 1:
        xs = [op(xs[i], xs[i + 1]) if i + 1 < len(xs) else xs[i] for i in range(0, len(xs), 2)]
    return xs[0]


def _attn_kernel(qt_ref, k_ref, fa_ref, vt_ref, o_ref, s_ref, mx_ref, *, ts, dh):
    qi = pl.program_id(2)
    tq = qt_ref.shape[1]
    heads = LANES // dh
    spq = tq // ts

    qt = qt_ref[...]
    rowi = lax.broadcasted_iota(jnp.int32, qt.shape, 0)
    qts = []
    for e in range(heads):
        own_f = (rowi // FBIAS_PIECES) == e
        qts.append(jnp.concatenate(
            [jnp.where((rowi // dh) == e, qt, jnp.zeros_like(qt)),
             jnp.where(own_f, -1.0, 0.0).astype(bf16)], axis=0))
    nrt = ts // SUBLANES

    def issue(j, lane0=0):
        jd, slot = j
        ks = pl.ds(pl.multiple_of(jd * ts, ts), ts)
        kb = jnp.concatenate([k_ref[ks, :], fa_ref[ks, :]], axis=1)
        for e in range(heads):
            s = _dot(kb, qts[e][:, lane0:])
            s_ref[slot, e, :, lane0:] = s
            if lane0 == 0:
                mx_ref[slot, e] = _tree([s[r * SUBLANES:(r + 1) * SUBLANES, :]
                                         for r in range(nrt)], jnp.maximum)

    ones_rows = (lax.broadcasted_iota(jnp.int32, (BF16_ROWS, ATTN_PAIR * ts), 0) == 0).astype(bf16)
    acc_rows = dh + SUBLANES

    def absorb(js, carry, key_offs):
        vtb = jnp.concatenate([vt_ref[jd] for jd, _ in js], axis=1)
        lane0 = 0 if key_offs is None else min(key_offs)
        out = []
        for e in range(heads):
            m_all, acc_all = carry[2 * e:2 * e + 2]
            m, acc = m_all[:, lane0:], acc_all[:, lane0:]
            if key_offs is None:
                tiles = None
                mx = _tree([mx_ref[slot, e] for _, slot in js], jnp.maximum)
            else:
                tiles = []
                for (_, slot), off in zip(js, key_offs):
                    kidx = lax.broadcasted_iota(jnp.int32, (ts, tq - lane0), 0) + off
                    qidx = lax.broadcasted_iota(jnp.int32, (ts, tq - lane0), 1) + lane0
                    sm = jnp.where(kidx <= qidx, s_ref[slot, e, :, lane0:], NEG_BIG)
                    tiles += [sm[r * SUBLANES:(r + 1) * SUBLANES, :] for r in range(nrt)]
                mx = _tree(tiles, jnp.maximum)
            m_new = jnp.maximum(m, _sublane_allreduce(mx, jnp.maximum))
            alpha = jnp.exp2(m - m_new)
            if tiles is None:
                tiles = [s_ref[slot, e, r * SUBLANES:(r + 1) * SUBLANES, :]
                         for _, slot in js for r in range(nrt)]
            p = jnp.concatenate([jnp.exp2(t - m_new) for t in tiles], axis=0).astype(bf16)
            lhs = jnp.concatenate([vtb[e * dh:(e + 1) * dh, :], ones_rows], axis=0)
            pv = _dot(lhs, p)[:acc_rows, :]
            acc = jnp.concatenate([alpha] * (acc_rows // SUBLANES), axis=0) * acc + pv
            if lane0:
                m_new = jnp.concatenate([m_all[:, :lane0], m_new], axis=1)
                acc = jnp.concatenate([acc_all[:, :lane0], acc], axis=1)
            out += [m_new, acc]
        return tuple(out)

    la = ATTN_LOOKAHEAD
    assert la + ATTN_PAIR <= spq and spq % ATTN_PAIR == 0

    def strip_id(j0, d):
        return (j0 + d, d % spq)

    def group(j0, carry, masked):
        last = None if not masked else spq
        for d0 in range(0, spq, ATTN_PAIR):
            ds = range(d0, d0 + ATTN_PAIR)
            for d in ds:
                if last is None:
                    issue(strip_id(j0, d + la))
                elif d + la < last:
                    issue(strip_id(j0, d + la), (d + la) // ATTN_PAIR * ATTN_PAIR * ts)
            carry = absorb([strip_id(j0, d) for d in ds], carry,
                           [d * ts for d in ds] if masked else None)
        return carry

    for d in range(la):
        issue(strip_id(0, d))
    init = []
    for e in range(heads):
        init += [jnp.full((SUBLANES, tq), NEG_BIG, f32), jnp.zeros((acc_rows, tq), f32)]
    def pair(t, c):
        return group((2 * t + 1) * spq, group(2 * t * spq, c, False), False)

    carry = lax.fori_loop(0, qi // 2, pair, tuple(init))
    carry = lax.fori_loop(qi - qi % 2, qi, lambda g, c: group(g * spq, c, False), carry)
    carry = group(qi * spq, carry, True)
    outs = []
    for e in range(heads):
        acc = carry[2 * e + 1]
        l = jnp.broadcast_to(acc[dh:dh + 1, :], (dh, tq))
        outs.append(acc[:dh, :] / l)
    o_ref[...] = jnp.concatenate(outs, axis=0).T.astype(o_ref.dtype)


def _attn(qt, k, fa, vt, n_heads, tq):
    B, D, S = qt.shape
    ts = vt.shape[-1]
    dh = D // n_heads
    heads = LANES // dh
    nhp = D // LANES
    assert heads * FBIAS_PIECES <= LANES and heads == 2
    kern = functools.partial(_attn_kernel, ts=ts, dh=dh)
    return pl.pallas_call(
        kern,
        grid=(B, nhp, S // tq),
        in_specs=[
            pl.BlockSpec((None, LANES, tq), lambda b, h, i: (b, h, i)),
            pl.BlockSpec((None, S, LANES), lambda b, h, i: (b, 0, h)),
            pl.BlockSpec((None, None, S, LANES), lambda b, h, i: (b, h, 0, 0)),
            pl.BlockSpec((None, S // ts, LANES, ts), lambda b, h, i: (b, 0, h, 0)),
        ],
        out_specs=pl.BlockSpec((None, tq, LANES), lambda b, h, i: (b, i, h)),
        out_shape=jax.ShapeDtypeStruct((B, S, D), bf16),
        scratch_shapes=[pltpu.VMEM((tq // ts, heads, ts, tq), f32),
                        pltpu.VMEM((tq // ts, heads, SUBLANES, tq), f32)],
        compiler_params=_cparams("parallel", "parallel", "arbitrary"),
        name="attn",
    )(qt, k, fa, vt)


def _fox_out_kernel(o_ref, h_ref, mod_ref, w_ref, out_ref):
    gate = mod_ref[2:3, :]
    for rows in _subtiles(o_ref.shape[0]):
        out_ref[rows, :] = h_ref[rows, :] + gate * _dot(o_ref[rows, :], w_ref[...])


def _fox_out(o, h, mod4, l, w_bf, tm):
    B, S, D = h.shape
    row = pl.BlockSpec((None, tm, D), lambda b, i: (b, i, 0))
    return pl.pallas_call(
        _fox_out_kernel,
        grid=(B, S // tm),
        in_specs=[row, row, _mod_spec(D)(l), _resident((D, D))],
        out_specs=row,
        out_shape=jax.ShapeDtypeStruct((B, S, D), f32),
        compiler_params=_cparams("parallel", "parallel"),
        name="fox_out",
    )(o, h, mod4, w_bf)


def kernel(x, c, norm_g, ada_w, ada_b, s5_w_in, s5_lam_re, s5_lam_im, s5_log_dt, s5_b_re, s5_b_im, s5_c_re, s5_c_im, s5_d, s5_w_glu, s5_w_out, fox_w_in, fox_b_f, fox_w_out, ffn_w_up, ffn_conv_w, ffn_conv_b, ffn_w_down, final_g):
    B, S, D = x.shape
    depth = norm_g.shape[0]
    n_heads = fox_b_f.shape[-1]
    tm = min(512, S)
    tmm = min(1024, S)
    fc = 256
    final = final_g.reshape(1, D)

    mod = _adaln(c, ada_w, ada_b)
    mod4 = mod.reshape(depth * 2, B, 3, D)

    h = x
    for i in range(depth):
        j = i // 2
        g_mix = norm_g[i, 0].reshape(1, D)
        g_ffn = norm_g[i, 1].reshape(1, D)
        if i % 2 == 0:
            u = _s5_in(h, mod4, 2 * i, g_mix, s5_w_in[j].astype(bf16), tmm)
            wcs, wout, wtz, adec = _s5_weights(s5_lam_re[j], s5_lam_im[j], s5_log_dt[j],
                                               s5_b_re[j], s5_b_im[j], s5_c_re[j], s5_c_im[j],
                                               S5_CHUNK)
            z = _s5_core(u, wcs, wout, wtz, adec, s5_d[j].reshape(1, D), S5_CHUNK)
            h = _s5_out(z, h, mod4, 2 * i, s5_w_glu[j].astype(bf16),
                        s5_w_out[j].astype(bf16), tmm)
        else:
            w_in = fox_w_in[j]
            wqt = w_in[:, :D].T.astype(bf16)
            wk = w_in[:, D:2 * D].astype(bf16)
            wvt = w_in[:, 2 * D:3 * D].T.astype(bf16)
            wf = jnp.pad(w_in[:, 3 * D:], ((0, 0), (0, LANES - n_heads))).astype(bf16)
            bfp = jnp.pad(fox_b_f[j], (0, LANES - n_heads)).reshape(1, LANES)
            ts = LANES
            q_scale = float((D // n_heads) ** -0.5 * LOG2E)
            qt, k, vt, fa = _fox_in(h, mod4, 2 * i, g_mix, wqt, wk, wvt, wf, bfp,
                                    _fbias_selector(n_heads), q_scale, tmm, ts)
            o = _attn(qt, k, fa, vt, n_heads, tq=min(512, S))
            h = _fox_out(o, h, mod4, 2 * i, fox_w_out[j].astype(bf16), tmm)
        h = _ffn(h, mod4, 2 * i + 1, g_ffn, ffn_w_up[i], ffn_conv_w[i], ffn_conv_b[i],
                 ffn_w_down[i], final, tm, fc, final_norm=(i == depth - 1))
    return h
```

```python
import functools

import jax
import jax.numpy as jnp
import numpy as np
from jax import lax
from jax.experimental import pallas as pl
from jax.experimental.pallas import tpu as pltpu

EPS = 1e-6
LANES = 128
SUBLANES = 8
BF16_ROWS = 16
VMEM_LIMIT = 56 * 1024 * 1024

S5_CHUNK = 4
S5_SEGMENTS = SUBLANES
S5_SCAN_UNROLL = 4
S5_SCAN_SETS = 2

f32 = jnp.float32
bf16 = jnp.bfloat16


def _cparams(*sem):
    return pltpu.CompilerParams(dimension_semantics=sem, vmem_limit_bytes=VMEM_LIMIT)


def _resident(shape):
    nd = len(shape)
    return pl.BlockSpec(shape, lambda *_: (0,) * nd, pipeline_mode=pl.Buffered(1))


def _dot(a, b):
    return jnp.dot(a, b, preferred_element_type=f32)


def _dot_nt(a, b):
    return lax.dot_general(a, b, (((1,), (1,)), ((), ())), preferred_element_type=f32)


SUB_ROWS = 256


def _subtiles(tm):
    sub = min(SUB_ROWS, tm)
    assert tm % sub == 0
    return [slice(k * sub, (k + 1) * sub) for k in range(tm // sub)]


def _split3(x):
    hi = x.astype(bf16)
    r1 = x - hi.astype(f32)
    mid = r1.astype(bf16)
    lo = (r1 - mid.astype(f32)).astype(bf16)
    return hi, mid, lo


def _modulate(x, g, shift, scale):
    ms = jnp.mean(x * x, axis=-1, keepdims=True)
    return x * lax.rsqrt(ms + EPS) * g * (1.0 + scale) + shift


def _silu(x):
    return x * (1.0 / (1.0 + jnp.exp(-x)))


def _gelu_tanh(x):
    c = 0.7978845608028654
    return 0.5 * x * (1.0 + jnp.tanh(c * (x + 0.044715 * (x * x * x))))


def _adaln_kernel(ct_ref, w_ref, b_ref, o_ref):
    ct = ct_ref[...]
    cs = _silu(ct)
    D = ct.shape[0]
    rc = min(SUB_ROWS, D)
    for b in range(ct.shape[1]):
        acc = b_ref[...]
        for r0 in range(0, D, rc):
            acc = acc + jnp.sum(w_ref[r0:r0 + rc, :] * cs[r0:r0 + rc, b:b + 1], axis=0,
                                keepdims=True)
        o_ref[b:b + 1, :] = acc


def _adaln(c, ada_w, ada_b):
    B, D = c.shape
    L = ada_w.shape[0] * ada_w.shape[1]
    N = ada_w.shape[-1]
    tn = 1024 if N % 1024 == 0 else 512
    w = ada_w.reshape(L, D, N)
    bias = ada_b.reshape(L, 1, N)
    return pl.pallas_call(
        _adaln_kernel,
        grid=(L, N // tn),
        in_specs=[
            pl.BlockSpec((D, B), lambda l, j: (0, 0)),
            pl.BlockSpec((None, D, tn), lambda l, j: (l, 0, j)),
            pl.BlockSpec((None, 1, tn), lambda l, j: (l, 0, j)),
        ],
        out_specs=pl.BlockSpec((None, B, tn), lambda l, j: (l, 0, j)),
        out_shape=jax.ShapeDtypeStruct((L, B, N), f32),
        compiler_params=_cparams("parallel", "parallel"),
        name="adaln",
    )(c.T, w, bias)


def _mod_spec(D):
    return lambda l: pl.BlockSpec((None, None, 3, D), lambda b, i: (l, b, 0, 0))


def _s5_in_kernel(x_ref, mod_ref, g_ref, w_ref, u_ref):
    g, shift, scale = g_ref[...], mod_ref[0:1, :], mod_ref[1:2, :]
    for rows in _subtiles(x_ref.shape[0]):
        hn = _modulate(x_ref[rows, :], g, shift, scale)
        u = _dot(hn.astype(bf16), w_ref[...])
        for k in range(u_ref.shape[0]):
            u_ref[k, rows, :] = u[:, k * LANES:(k + 1) * LANES]


def _s5_in(x, mod4, l, g, w_bf, tm):
    B, S, D = x.shape
    return pl.pallas_call(
        _s5_in_kernel,
        grid=(B, S // tm),
        in_specs=[
            pl.BlockSpec((None, tm, D), lambda b, i: (b, i, 0)),
            _mod_spec(D)(l),
            _resident((1, D)),
            _resident((D, D)),
        ],
        out_specs=pl.BlockSpec((None, D // LANES, tm, LANES), lambda b, i: (b, 0, i, 0)),
        out_shape=jax.ShapeDtypeStruct((B, D // LANES, S, LANES), f32),
        compiler_params=_cparams("parallel", "parallel"),
        name="s5_in",
    )(x, mod4, g, w_bf)


def _s5_weights_kernel(lre_ref, lim_ref, ldt_ref, br_ref, bi_ref, cr_ref, ci_ref,
                       wcs_ref, wout_ref, wtz_ref, adec_ref, *, lc, cg, p_state):
    lre = lre_ref[...]
    lim = lim_ref[...]
    dt = jnp.exp(ldt_ref[...])
    nl = lre.shape[1]
    mag = jnp.exp(lre * dt)
    lam_r = mag * jnp.cos(lim * dt)
    lam_i = mag * jnp.sin(lim * dt)
    num_r = lam_r - 1.0
    den = lre * lre + lim * lim
    k_r = (num_r * lre + lam_i * lim) / den
    k_i = (lam_i * lre - num_r * lim) / den
    br = br_ref[...]
    bi = bi_ref[...]
    bb_r = k_r * br - k_i * bi
    bb_i = k_r * bi + k_i * br
    cr = cr_ref[...]
    ci = ci_ref[...]

    ngrp = nl // p_state
    rows = ngrp * cg
    row_grp = lax.broadcasted_iota(jnp.int32, (rows, nl), 0) // cg
    lane_grp = lax.broadcasted_iota(jnp.int32, (rows, nl), 1) // p_state
    own = row_grp == lane_grp

    def slab(xr, xi):
        tr = jnp.where(own, jnp.tile(xr, (ngrp, 1)), 0.0)
        ti = jnp.where(own, jnp.tile(xi, (ngrp, 1)), 0.0)
        return jnp.concatenate([tr, ti], axis=1)

    pw = [(jnp.ones_like(lam_r), jnp.zeros_like(lam_r))]
    for _ in range(lc):
        pr, pi = pw[-1]
        pw.append((pr * lam_r - pi * lam_i, pr * lam_i + pi * lam_r))
    adec_ref[0:1, :] = pw[lc][0]
    adec_ref[1:2, :] = pw[lc][1]

    for s in range(lc):
        pr, pi = pw[lc - 1 - s]
        wcs_ref[s * rows:(s + 1) * rows, :] = slab(pr * bb_r - pi * bb_i,
                                                   pr * bb_i + pi * bb_r).astype(bf16)

    wq = []
    for t in range(lc + 1):
        pr, pi = pw[t]
        wq.append(slab(cr * pr - ci * pi, -(cr * pi + ci * pr)))
    for r in range(lc):
        wout_ref[r * rows:(r + 1) * rows, :] = wq[r + 1].astype(bf16)

    bsl = _split3(slab(bb_r, bb_i))
    kt = []
    for t in range(lc):
        q = _split3(wq[t])
        acc = _dot_nt(bsl[0], q[0])
        acc += _dot_nt(bsl[0], q[1]) + _dot_nt(bsl[1], q[0])
        acc += _dot_nt(bsl[1], q[1]) + _dot_nt(bsl[0], q[2]) + _dot_nt(bsl[2], q[0])
        kt.append(acc)
    zero = jnp.zeros((rows, rows), f32)
    for s in range(lc):
        for r in range(lc):
            blk = kt[r - s] if r >= s else zero
            wtz_ref[s * rows:(s + 1) * rows, r * rows:(r + 1) * rows] = blk.astype(bf16)


def _s5_weights(lam_re, lam_im, log_dt, b_re, b_im, c_re, c_im, lc):
    G, P = lam_re.shape
    Cg = b_re.shape[-1]
    gb = LANES // Cg
    nb = G // gb
    nl = gb * P
    rows = gb * Cg
    lre = lam_re.reshape(nb, 1, nl)
    lim = lam_im.reshape(nb, 1, nl)
    ldt = jnp.repeat(log_dt, P).reshape(nb, 1, nl)
    brt = b_re.reshape(nb, gb, P, Cg).transpose(0, 3, 1, 2).reshape(nb, Cg, nl)
    bit = b_im.reshape(nb, gb, P, Cg).transpose(0, 3, 1, 2).reshape(nb, Cg, nl)
    crt = c_re.reshape(nb, gb, Cg, P).transpose(0, 2, 1, 3).reshape(nb, Cg, nl)
    cit = c_im.reshape(nb, gb, Cg, P).transpose(0, 2, 1, 3).reshape(nb, Cg, nl)
    vec = pl.BlockSpec((None, 1, nl), lambda k: (k, 0, 0))
    mat = pl.BlockSpec((None, Cg, nl), lambda k: (k, 0, 0))
    return pl.pallas_call(
        functools.partial(_s5_weights_kernel, lc=lc, cg=Cg, p_state=P),
        grid=(nb,),
        in_specs=[vec, vec, vec, mat, mat, mat, mat],
        out_specs=[
            pl.BlockSpec((None, lc * rows, 2 * nl), lambda k: (k, 0, 0)),
            pl.BlockSpec((None, lc * rows, 2 * nl), lambda k: (k, 0, 0)),
            pl.BlockSpec((None, lc * rows, lc * rows), lambda k: (k, 0, 0)),
            pl.BlockSpec((None, 2, nl), lambda k: (k, 0, 0)),
        ],
        out_shape=[
            jax.ShapeDtypeStruct((nb, lc * rows, 2 * nl), bf16),
            jax.ShapeDtypeStruct((nb, lc * rows, 2 * nl), bf16),
            jax.ShapeDtypeStruct((nb, lc * rows, lc * rows), bf16),
            jax.ShapeDtypeStruct((nb, 2, nl), f32),
        ],
        compiler_params=_cparams("parallel"),
        name="s5_weights",
    )(lre, lim, ldt, brt, bit, crt, cit)


def _s5_core_kernel(u_ref, wcs_ref, wout_ref, wtz_ref, adec_ref, d_ref, z_ref,
                    st_ref, y1_ref, y_ref, *, lc, nseg, seg_len):
    nsets, nslab = st_ref.shape[0], st_ref.shape[1]
    half = nslab // 2
    seg_tok = seg_len * lc
    sl = seg_len // nsets

    def phase_a(seg, carry):
        tok0 = seg * seg_tok
        x = jnp.concatenate(
            [u_ref[pl.ds(tok0 + s, seg_len, stride=lc), :] for s in range(lc)], axis=1)
        xb = x.astype(bf16)
        e = _dot(xb, wcs_ref[...])
        for h in range(nsets):
            for j in range(nslab):
                st_ref[h, j, pl.ds(seg, sl, stride=nseg), :] = \
                    e[h * sl:(h + 1) * sl, j * LANES:(j + 1) * LANES]
        y1_ref[pl.ds(seg * seg_len, seg_len), :] = _dot(xb, wtz_ref[...])
        return carry

    for seg in range(nseg):
        phase_a(seg, 0)

    a_r = [jnp.broadcast_to(adec_ref[0:1, j * LANES:(j + 1) * LANES], (nseg, LANES))
           for j in range(half)]
    a_i = [jnp.broadcast_to(adec_ref[1:2, j * LANES:(j + 1) * LANES], (nseg, LANES))
           for j in range(half)]

    def step(i, hs, store):
        out = []
        rows = pl.ds(pl.multiple_of(i * nseg, nseg), nseg)
        for h in range(nsets):
            for j in range(half):
                hr, hi = hs[h * nslab + 2 * j], hs[h * nslab + 2 * j + 1]
                er = st_ref[h, j, rows, :]
                ei = st_ref[h, half + j, rows, :]
                if store:
                    st_ref[h, j, rows, :] = hr
                    st_ref[h, half + j, rows, :] = hi
                out.append(a_r[j] * hr - a_i[j] * hi + er)
                out.append(a_r[j] * hi + a_i[j] * hr + ei)
        return tuple(out)

    zeros = tuple(jnp.zeros((nseg, LANES), f32) for _ in range(nsets * nslab))
    ends = lax.fori_loop(0, sl, lambda i, hs: step(i, hs, False), zeros, unroll=S5_SCAN_UNROLL)

    starts = [None] * (nsets * nslab)
    for j in range(half):
        pr, pi = a_r[j][0:1, :], a_i[j][0:1, :]
        n = sl
        while n > 1:
            pr, pi = pr * pr - pi * pi, 2.0 * pr * pi
            n //= 2
        cr, ci = jnp.zeros((1, LANES), f32), jnp.zeros((1, LANES), f32)
        sr = [[] for _ in range(nsets)]
        si = [[] for _ in range(nsets)]
        for k in range(nseg):
            for h in range(nsets):
                sr[h].append(cr)
                si[h].append(ci)
                er = ends[h * nslab + 2 * j][k:k + 1, :]
                ei = ends[h * nslab + 2 * j + 1][k:k + 1, :]
                cr, ci = pr * cr - pi * ci + er, pr * ci + pi * cr + ei
        for h in range(nsets):
            starts[h * nslab + 2 * j] = jnp.concatenate(sr[h], axis=0)
            starts[h * nslab + 2 * j + 1] = jnp.concatenate(si[h], axis=0)
    lax.fori_loop(0, sl, lambda i, hs: step(i, hs, True), tuple(starts), unroll=S5_SCAN_UNROLL)

    d = d_ref[...]

    def phase_c(seg):
        h = jnp.concatenate(
            [jnp.concatenate([st_ref[hh, j, pl.ds(seg, sl, stride=nseg), :]
                              for hh in range(nsets)], axis=0) for j in range(nslab)], axis=1)
        y = _dot_nt(h.astype(bf16), wout_ref[...])
        y = y + y1_ref[pl.ds(seg * seg_len, seg_len), :]
        tok0 = seg * seg_tok
        for r in range(lc):
            y_ref[pl.ds(tok0 + r, seg_len, stride=lc), :] = y[:, r * LANES:(r + 1) * LANES]

    def phase_d(seg):
        rows = pl.ds(seg * seg_tok, seg_tok)
        z_ref[rows, :] = _gelu_tanh(y_ref[rows, :] + d * u_ref[rows, :]).astype(z_ref.dtype)

    phase_c(0)
    for seg in range(nseg):
        if seg + 1 < nseg:
            phase_c(seg + 1)
        phase_d(seg)


def _s5_core(u, wcs, wout, wtz, adec, d_skip, lc):
    B, nb, S, _ = u.shape
    nseg = S5_SEGMENTS
    m = S // lc
    seg_len = m // nseg
    assert seg_len * nseg * lc == S and seg_len & (seg_len - 1) == 0
    nslab = wcs.shape[-1] // LANES
    kw = wcs.shape[1]
    kern = functools.partial(_s5_core_kernel, lc=lc, nseg=nseg, seg_len=seg_len)
    return pl.pallas_call(
        kern,
        grid=(nb, B),
        in_specs=[
            pl.BlockSpec((None, None, S, LANES), lambda k, b: (b, k, 0, 0)),
            pl.BlockSpec((None, kw, nslab * LANES), lambda k, b: (k, 0, 0)),
            pl.BlockSpec((None, kw, nslab * LANES), lambda k, b: (k, 0, 0)),
            pl.BlockSpec((None, kw, kw), lambda k, b: (k, 0, 0)),
            pl.BlockSpec((None, 2, adec.shape[-1]), lambda k, b: (k, 0, 0)),
            pl.BlockSpec((1, LANES), lambda k, b: (0, k)),
        ],
        out_specs=pl.BlockSpec((None, None, S, LANES), lambda k, b: (b, k, 0, 0)),
        out_shape=jax.ShapeDtypeStruct((B, nb, S, LANES), bf16),
        scratch_shapes=[
            pltpu.VMEM((S5_SCAN_SETS, nslab, nseg * seg_len // S5_SCAN_SETS, LANES), f32),
            pltpu.VMEM((m, kw), f32),
            pltpu.VMEM((S, LANES), f32),
        ],
        compiler_params=_cparams("parallel", "parallel"),
        name="s5_core",
    )(u, wcs, wout, wtz, adec, d_skip)


def _s5_out_kernel(z_ref, x_ref, mod_ref, wg_ref, wo_ref, h_ref):
    subs = _subtiles(x_ref.shape[0])
    gate = mod_ref[2:3, :]

    def z_rows(rows):
        return jnp.concatenate([z_ref[k, rows, :] for k in range(z_ref.shape[0])], axis=1)

    def glu_logits(rows):
        return _dot(z_rows(rows), wg_ref[...])

    g_next = glu_logits(subs[0])
    for k, rows in enumerate(subs):
        g = g_next
        if k + 1 < len(subs):
            g_next = glu_logits(subs[k + 1])
        z2 = z_rows(rows).astype(f32) * (1.0 / (1.0 + jnp.exp(-g)))
        m = _dot(z2.astype(bf16), wo_ref[...])
        h_ref[rows, :] = x_ref[rows, :] + gate * m


def _s5_out(z, x, mod4, l, wg_bf, wo_bf, tm):
    B, S, D = x.shape
    row = pl.BlockSpec((None, tm, D), lambda b, i: (b, i, 0))
    zblk = pl.BlockSpec((None, D // LANES, tm, LANES), lambda b, i: (b, 0, i, 0))
    return pl.pallas_call(
        _s5_out_kernel,
        grid=(B, S // tm),
        in_specs=[zblk, row, _mod_spec(D)(l), _resident((D, D)), _resident((D, D))],
        out_specs=row,
        out_shape=jax.ShapeDtypeStruct((B, S, D), f32),
        compiler_params=_cparams("parallel", "parallel"),
        name="s5_out",
    )(z, x, mod4, wg_bf, wo_bf)


def _ffn_kernel(h_ref, halo_ref, mod_ref, g_ref, wu_ref, wd_ref, cw_ref, cb_ref,
                fg_ref, o_ref, hn_ref, a_ref, b_ref, gt_ref, *, fc, final_norm):
    i = pl.program_id(1)
    tm = h_ref.shape[0]
    hb = halo_ref.shape[0]
    taps, F = cw_ref.shape
    nc = F // fc
    g, shift, scale = g_ref[...], mod_ref[0:1, :], mod_ref[1:2, :]
    x = h_ref[...]
    keep = (i > 0).astype(f32)
    hn_ref[0:hb, :] = (_modulate(halo_ref[...], g, shift, scale) * keep).astype(bf16)
    a_ref[0, 0:hb, :] = _dot(hn_ref[0:hb, :], wu_ref[:, 0:fc])
    for rows in _subtiles(tm):
        hrows = slice(hb + rows.start, hb + rows.stop)
        hn = _modulate(h_ref[rows, :], g, shift, scale).astype(bf16)
        hn_ref[hrows, :] = hn
        a_ref[0, hrows, :] = _dot(hn, wu_ref[:, 0:fc])
        b_ref[0, rows, :] = _dot(hn, wu_ref[:, F:F + fc])

    def up(j, slot):
        cols = slice(j * fc, (j + 1) * fc)
        a_ref[slot] = _dot(hn_ref[...], wu_ref[:, cols])
        b_ref[slot] = _dot(hn_ref[hb:, :], wu_ref[:, F + j * fc:F + (j + 1) * fc])

    def gate(j, slot):
        cols = slice(j * fc, (j + 1) * fc)
        cw = cw_ref[:, cols]
        ac = cb_ref[:, cols]
        for t in range(taps):
            off = hb - (taps - 1) + t
            ac = ac + cw[t:t + 1, :] * a_ref[slot, off:off + tm, :]
        gt_ref[:, cols] = (_silu(ac) * b_ref[slot]).astype(bf16)

    for j in range(nc):
        if j + 1 < nc:
            up(j + 1, (j + 1) % 2)
        gate(j, j % 2)
    out = x + mod_ref[2:3, :] * _dot(gt_ref[...], wd_ref[...])
    if final_norm:
        ms = jnp.mean(out * out, axis=-1, keepdims=True)
        out = out * lax.rsqrt(ms + EPS) * fg_ref[...]
    o_ref[...] = out


def _ffn(h, mod4, l, g, w_up, conv_w, conv_b, w_down, final_g, tm, fc, final_norm):
    B, S, D = h.shape
    F = w_down.shape[0]
    taps = conv_w.shape[0]
    assert F % fc == 0 and F % LANES == 0
    hb = BF16_ROWS
    per = tm // hb
    kern = functools.partial(_ffn_kernel, fc=fc, final_norm=final_norm)
    return pl.pallas_call(
        kern,
        grid=(B, S // tm),
        in_specs=[
            pl.BlockSpec((None, tm, D), lambda b, i: (b, i, 0)),
            pl.BlockSpec((None, hb, D), lambda b, i: (b, jnp.maximum(i * per - 1, 0), 0)),
            _mod_spec(D)(l),
            _resident((1, D)),
            _resident((D, 2 * F)),
            _resident((F, D)),
            _resident((taps, F)),
            _resident((1, F)),
            _resident((1, D)),
        ],
        out_specs=pl.BlockSpec((None, tm, D), lambda b, i: (b, i, 0)),
        out_shape=jax.ShapeDtypeStruct((B, S, D), f32),
        scratch_shapes=[
            pltpu.VMEM((hb + tm, D), bf16),
            pltpu.VMEM((2, hb + tm, fc), f32),
            pltpu.VMEM((2, tm, fc), f32),
            pltpu.VMEM((tm, F), bf16),
        ],
        compiler_params=_cparams("parallel", "parallel"),
        name="ffn",
    )(h, h, mod4, g, w_up.astype(bf16), w_down.astype(bf16), conv_w, conv_b.reshape(1, F),
      final_g)


LOG2E = 1.4426950408889634


FBIAS_PIECES = 3


def _fbias_selector(n_heads):
    heads = 2
    sel = np.zeros((FBIAS_PIECES * LANES, (n_heads // heads) * LANES), np.float32)
    for h in range(n_heads):
        for p in range(FBIAS_PIECES):
            sel[p * LANES + h, (h // heads) * LANES + (h % heads) * FBIAS_PIECES + p] = 1.0
    return jnp.asarray(sel, dtype=bf16)


def _fox_in_kernel(h_ref, mod_ref, g_ref, wqt_ref, wk_ref, wvt_ref, wf_ref, bf_ref, sel_ref,
                   qt_ref, k_ref, vt_ref, fa_ref, carry_ref, *, q_scale, ts):
    i = pl.program_id(1)
    tm, D = h_ref.shape

    @pl.when(i == 0)
    def _():
        carry_ref[...] = jnp.zeros_like(carry_ref)

    g, shift, scale = g_ref[...], mod_ref[0:1, :], mod_ref[1:2, :]
    subs = _subtiles(tm)
    sub = subs[0].stop
    r = lax.broadcasted_iota(jnp.int32, (sub, sub), 0)
    c = lax.broadcasted_iota(jnp.int32, (sub, sub), 1)
    tri = (c <= r).astype(bf16)
    carry = carry_ref[...]
    for rows in subs:
        hn = _modulate(h_ref[rows, :], g, shift, scale).astype(bf16)
        k_ref[rows, :] = _dot(hn, wk_ref[...]).astype(bf16)
        qt_ref[:, rows] = (_dot_nt(wqt_ref[...], hn) * q_scale).astype(bf16)
        vt = _dot_nt(wvt_ref[...], hn).astype(bf16)
        for cc in range(sub // ts):
            vt_ref[rows.start // ts + cc] = vt[:, cc * ts:(cc + 1) * ts]

        fl = _dot(hn, wf_ref[...]) + bf_ref[...]
        lf = jnp.minimum(fl, 0.0) - jnp.log(1.0 + jnp.exp(-jnp.abs(fl)))
        p0, p1, p2 = _split3(lf)
        cs = _dot(tri, p0) + _dot(tri, p1) + _dot(tri, p2) + carry
        carry = cs[sub - 1:sub, :]
        fa = _dot(jnp.concatenate(_split3(cs * LOG2E), axis=1), sel_ref[...]).astype(bf16)
        for hp in range(fa_ref.shape[0]):
            fa_ref[hp, rows, :] = fa[:, hp * LANES:(hp + 1) * LANES]
    carry_ref[...] = carry


def _fox_in(h, mod4, l, g, wqt_bf, wk_bf, wvt_bf, wf_bf, bf_pad, sel, q_scale, tm, ts):
    B, S, D = h.shape
    nhp = sel.shape[1] // LANES
    return pl.pallas_call(
        functools.partial(_fox_in_kernel, q_scale=q_scale, ts=ts),
        grid=(B, S // tm),
        in_specs=[
            pl.BlockSpec((None, tm, D), lambda b, i: (b, i, 0)),
            _mod_spec(D)(l),
            _resident((1, D)),
            _resident((D, D)),
            _resident((D, D)),
            _resident((D, D)),
            _resident((D, LANES)),
            _resident((1, LANES)),
            _resident(sel.shape),
        ],
        out_specs=[
            pl.BlockSpec((None, D, tm), lambda b, i: (b, 0, i)),
            pl.BlockSpec((None, tm, D), lambda b, i: (b, i, 0)),
            pl.BlockSpec((None, tm // ts, D, ts), lambda b, i: (b, i, 0, 0)),
            pl.BlockSpec((None, nhp, tm, LANES), lambda b, i: (b, 0, i, 0)),
        ],
        out_shape=[
            jax.ShapeDtypeStruct((B, D, S), bf16),
            jax.ShapeDtypeStruct((B, S, D), bf16),
            jax.ShapeDtypeStruct((B, S // ts, D, ts), bf16),
            jax.ShapeDtypeStruct((B, nhp, S, LANES), bf16),
        ],
        scratch_shapes=[pltpu.VMEM((1, LANES), f32)],
        compiler_params=_cparams("parallel", "arbitrary"),
        name="fox_in",
    )(h, mod4, g, wqt_bf, wk_bf, wvt_bf, wf_bf, bf_pad, sel)


NEG_BIG = -1e30
ATTN_PAIR = 2


def _sublane_allreduce(x, op):
    for sh in (4, 2, 1):
        x = op(x, pltpu.roll(x, sh, axis=0))
    return x


def _tree(xs, op):
    while len(xs) > 1:
        xs = [op(xs[i], xs[i + 1]) if i + 1 < len(xs) else xs[i] for i in range(0, len(xs), 2)]
    return xs[0]


def _attn_kernel(qta_ref, qtb_ref, k_ref, fa_ref, vt_ref, o_ref, s_ref, mx_ref, *, ts, dh):
    i = pl.program_id(2)
    nq = 2 * pl.num_programs(2)
    _attn_block(i, qta_ref, k_ref, fa_ref, vt_ref, o_ref.at[0], s_ref, mx_ref, ts=ts, dh=dh)
    _attn_block(nq - 1 - i, qtb_ref, k_ref, fa_ref, vt_ref, o_ref.at[1], s_ref, mx_ref,
                ts=ts, dh=dh)


def _attn_block(qi, qt_ref, k_ref, fa_ref, vt_ref, o_ref, s_ref, mx_ref, *, ts, dh):
    tq = qt_ref.shape[1]
    heads = LANES // dh
    spq = tq // ts

    qt = qt_ref[...]
    rowi = lax.broadcasted_iota(jnp.int32, qt.shape, 0)
    qts = []
    for e in range(heads):
        own_f = (rowi // FBIAS_PIECES) == e
        qts.append(jnp.concatenate(
            [jnp.where((rowi // dh) == e, qt, jnp.zeros_like(qt)),
             jnp.where(own_f, -1.0, 0.0).astype(bf16)], axis=0))
    nrt = ts // SUBLANES

    def issue(g, bank):
        gk = spq * ts
        ks = pl.ds(pl.multiple_of(g * gk, gk), gk)
        kb = jnp.concatenate([k_ref[ks, :], fa_ref[ks, :]], axis=1)
        for e in range(heads):
            s = _dot(kb, qts[e])
            for d in range(spq):
                sd = s[d * ts:(d + 1) * ts, :]
                s_ref[bank, d, e] = sd
                mx_ref[bank, d, e] = _tree([sd[r * SUBLANES:(r + 1) * SUBLANES, :]
                                            for r in range(nrt)], jnp.maximum)

    ones_rows = (lax.broadcasted_iota(jnp.int32, (BF16_ROWS, ATTN_PAIR * ts), 0) == 0).astype(bf16)
    acc_rows = dh + SUBLANES

    def absorb(g, bank, ds, carry, masked):
        vtb = jnp.concatenate([vt_ref[g * spq + d] for d in ds], axis=1)
        lane0 = ds[0] * ts if masked else 0
        out = []
        for e in range(heads):
            m_all, acc_all = carry[2 * e:2 * e + 2]
            m, acc = m_all[:, lane0:], acc_all[:, lane0:]
            if not masked:
                tiles = None
                mx = _tree([mx_ref[bank, d, e] for d in ds], jnp.maximum)
            else:
                tiles = []
                for d in ds:
                    kidx = lax.broadcasted_iota(jnp.int32, (ts, tq - lane0), 0) + d * ts
                    qidx = lax.broadcasted_iota(jnp.int32, (ts, tq - lane0), 1) + lane0
                    sm = jnp.where(kidx <= qidx, s_ref[bank, d, e, :, lane0:], NEG_BIG)
                    tiles += [sm[r * SUBLANES:(r + 1) * SUBLANES, :] for r in range(nrt)]
                mx = _tree(tiles, jnp.maximum)
            m_new = jnp.maximum(m, _sublane_allreduce(mx, jnp.maximum))
            alpha = jnp.exp2(m - m_new)
            if tiles is None:
                tiles = [s_ref[bank, d, e, r * SUBLANES:(r + 1) * SUBLANES, :]
                         for d in ds for r in range(nrt)]
            p = jnp.concatenate([jnp.exp2(t - m_new) for t in tiles], axis=0).astype(bf16)
            lhs = jnp.concatenate([vtb[e * dh:(e + 1) * dh, :], ones_rows], axis=0)
            pv = _dot(lhs, p)[:acc_rows, :]
            acc = jnp.concatenate([alpha] * (acc_rows // SUBLANES), axis=0) * acc + pv
            if lane0:
                m_new = jnp.concatenate([m_all[:, :lane0], m_new], axis=1)
                acc = jnp.concatenate([acc_all[:, :lane0], acc], axis=1)
            out += [m_new, acc]
        return tuple(out)

    assert spq % ATTN_PAIR == 0

    def group(g, bank, carry, masked):
        for d0 in range(0, spq, ATTN_PAIR):
            carry = absorb(g, bank, list(range(d0, d0 + ATTN_PAIR)), carry, masked)
        return carry

    init = []
    for e in range(heads):
        init += [jnp.full((SUBLANES, tq), NEG_BIG, f32), jnp.zeros((acc_rows, tq), f32)]
    issue(0, 0)

    def pair(t, c):
        g = 2 * t
        issue(g + 1, 1)
        c = group(g, 0, c, False)
        issue(g + 2, 0)
        return group(g + 1, 1, c, False)

    carry = lax.fori_loop(0, qi // 2, pair, tuple(init))

    def tail_even(c):
        return group(qi, 0, c, True)

    def tail_odd(c):
        issue(qi, 1)
        return group(qi, 1, group(qi - 1, 0, c, False), True)

    carry = lax.cond(qi % 2 == 0, tail_even, tail_odd, carry)
    outs = []
    for e in range(heads):
        acc = carry[2 * e + 1]
        l = jnp.broadcast_to(acc[dh:dh + 1, :], (dh, tq))
        outs.append(acc[:dh, :] / l)
    o_ref[...] = jnp.concatenate(outs, axis=0).T.astype(o_ref.dtype)


def _attn(qt, k, fa, vt, n_heads, tq):
    B, D, S = qt.shape
    ts = vt.shape[-1]
    dh = D // n_heads
    heads = LANES // dh
    nhp = D // LANES
    assert heads * FBIAS_PIECES <= LANES and heads == 2
    nq = S // tq
    assert nq % 2 == 0
    kern = functools.partial(_attn_kernel, ts=ts, dh=dh)
    return pl.pallas_call(
        kern,
        grid=(B, nhp, nq // 2),
        in_specs=[
            pl.BlockSpec((None, LANES, tq), lambda b, h, i: (b, h, i)),
            pl.BlockSpec((None, LANES, tq), lambda b, h, i: (b, h, nq - 1 - i)),
            pl.BlockSpec((None, S, LANES), lambda b, h, i: (b, 0, h)),
            pl.BlockSpec((None, None, S, LANES), lambda b, h, i: (b, h, 0, 0)),
            pl.BlockSpec((None, S // ts, LANES, ts), lambda b, h, i: (b, 0, h, 0)),
        ],
        out_specs=pl.BlockSpec((None, None, 2, tq, LANES), lambda b, h, i: (b, i, 0, 0, h)),
        out_shape=jax.ShapeDtypeStruct((B, nq // 2, 2, tq, D), bf16),
        scratch_shapes=[pltpu.VMEM((2, tq // ts, heads, ts, tq), f32),
                        pltpu.VMEM((2, tq // ts, heads, SUBLANES, tq), f32)],
        compiler_params=_cparams("parallel", "parallel", "arbitrary"),
        name="attn",
    )(qt, qt, k, fa, vt)


def _fox_out_kernel(o_ref, h_ref, mod_ref, w_ref, out_ref):
    gate = mod_ref[2:3, :]
    for rows in _subtiles(o_ref.shape[0]):
        out_ref[rows, :] = h_ref[rows, :] + gate * _dot(o_ref[rows, :], w_ref[...])


def _fox_out(o, h, mod4, l, w_bf):
    B, S, D = h.shape
    tm = o.shape[3]
    nq = S // tm
    row = pl.BlockSpec((None, tm, D), lambda b, i: (b, i, 0))
    orow = pl.BlockSpec(
        (None, None, None, tm, D),
        lambda b, i: (b, jnp.where(i < nq // 2, i, nq - 1 - i), (i >= nq // 2).astype(jnp.int32),
                      0, 0))
    return pl.pallas_call(
        _fox_out_kernel,
        grid=(B, S // tm),
        in_specs=[orow, row, _mod_spec(D)(l), _resident((D, D))],
        out_specs=row,
        out_shape=jax.ShapeDtypeStruct((B, S, D), f32),
        compiler_params=_cparams("parallel", "parallel"),
        name="fox_out",
    )(o, h, mod4, w_bf)


def kernel(x, c, norm_g, ada_w, ada_b, s5_w_in, s5_lam_re, s5_lam_im, s5_log_dt, s5_b_re, s5_b_im, s5_c_re, s5_c_im, s5_d, s5_w_glu, s5_w_out, fox_w_in, fox_b_f, fox_w_out, ffn_w_up, ffn_conv_w, ffn_conv_b, ffn_w_down, final_g):
    B, S, D = x.shape
    depth = norm_g.shape[0]
    n_heads = fox_b_f.shape[-1]
    tm = min(1024, S)
    tmm = min(1024, S)
    fc = 256
    final = final_g.reshape(1, D)

    mod = _adaln(c, ada_w, ada_b)
    mod4 = mod.reshape(depth * 2, B, 3, D)

    h = x
    for i in range(depth):
        j = i // 2
        g_mix = norm_g[i, 0].reshape(1, D)
        g_ffn = norm_g[i, 1].reshape(1, D)
        if i % 2 == 0:
            u = _s5_in(h, mod4, 2 * i, g_mix, s5_w_in[j].astype(bf16), tmm)
            wcs, wout, wtz, adec = _s5_weights(s5_lam_re[j], s5_lam_im[j], s5_log_dt[j],
                                               s5_b_re[j], s5_b_im[j], s5_c_re[j], s5_c_im[j],
                                               S5_CHUNK)
            z = _s5_core(u, wcs, wout, wtz, adec, s5_d[j].reshape(1, D), S5_CHUNK)
            h = _s5_out(z, h, mod4, 2 * i, s5_w_glu[j].astype(bf16),
                        s5_w_out[j].astype(bf16), tmm)
        else:
            w_in = fox_w_in[j]
            wqt = w_in[:, :D].T.astype(bf16)
            wk = w_in[:, D:2 * D].astype(bf16)
            wvt = w_in[:, 2 * D:3 * D].T.astype(bf16)
            wf = jnp.pad(w_in[:, 3 * D:], ((0, 0), (0, LANES - n_heads))).astype(bf16)
            bfp = jnp.pad(fox_b_f[j], (0, LANES - n_heads)).reshape(1, LANES)
            ts = LANES
            q_scale = float((D // n_heads) ** -0.5 * LOG2E)
            qt, k, vt, fa = _fox_in(h, mod4, 2 * i, g_mix, wqt, wk, wvt, wf, bfp,
                                    _fbias_selector(n_heads), q_scale, tmm, ts)
            o = _attn(qt, k, fa, vt, n_heads, tq=min(512, S))
            h = _fox_out(o, h, mod4, 2 * i, fox_w_out[j].astype(bf16))
        h = _ffn(h, mod4, 2 * i + 1, g_ffn, ffn_w_up[i], ffn_conv_w[i], ffn_conv_b[i],
                 ffn_w_down[i], final, tm, fc, final_norm=(i == depth - 1))
    return h
```

```python
import functools

import jax
import jax.numpy as jnp
import numpy as np
from jax import lax
from jax.experimental import pallas as pl
from jax.experimental.pallas import tpu as pltpu

EPS = 1e-6
LANES = 128
SUBLANES = 8
BF16_ROWS = 16
VMEM_LIMIT = 56 * 1024 * 1024

S5_CHUNK = 4
S5_SEGMENTS = SUBLANES
S5_SCAN_UNROLL = 4
S5_SCAN_SETS = 2
S5_MM_SEGMENTS = 4

f32 = jnp.float32
bf16 = jnp.bfloat16


def _cparams(*sem):
    return pltpu.CompilerParams(dimension_semantics=sem, vmem_limit_bytes=VMEM_LIMIT)


def _resident(shape):
    nd = len(shape)
    return pl.BlockSpec(shape, lambda *_: (0,) * nd, pipeline_mode=pl.Buffered(1))


def _dot(a, b):
    return jnp.dot(a, b, preferred_element_type=f32)


def _dot_nt(a, b):
    return lax.dot_general(a, b, (((1,), (1,)), ((), ())), preferred_element_type=f32)


SUB_ROWS = 256


def _subtiles(tm):
    sub = min(SUB_ROWS, tm)
    assert tm % sub == 0
    return [slice(k * sub, (k + 1) * sub) for k in range(tm // sub)]


def _split3(x):
    hi = x.astype(bf16)
    r1 = x - hi.astype(f32)
    mid = r1.astype(bf16)
    lo = (r1 - mid.astype(f32)).astype(bf16)
    return hi, mid, lo


def _modulate(x, g, shift, scale):
    ms = jnp.mean(x * x, axis=-1, keepdims=True)
    return x * lax.rsqrt(ms + EPS) * g * (1.0 + scale) + shift


def _silu(x):
    return x * (1.0 / (1.0 + jnp.exp(-x)))


def _gelu_tanh(x):
    c = 0.7978845608028654
    return 0.5 * x * (1.0 + jnp.tanh(c * (x + 0.044715 * (x * x * x))))


def _adaln_kernel(ct_ref, w_ref, b_ref, o_ref):
    ct = ct_ref[...]
    cs = _silu(ct)
    D = ct.shape[0]
    rc = min(SUB_ROWS, D)
    for b in range(ct.shape[1]):
        acc = b_ref[...]
        for r0 in range(0, D, rc):
            acc = acc + jnp.sum(w_ref[r0:r0 + rc, :] * cs[r0:r0 + rc, b:b + 1], axis=0,
                                keepdims=True)
        o_ref[b:b + 1, :] = acc


def _adaln(c, ada_w, ada_b):
    B, D = c.shape
    L = ada_w.shape[0] * ada_w.shape[1]
    N = ada_w.shape[-1]
    tn = 1024 if N % 1024 == 0 else 512
    w = ada_w.reshape(L, D, N)
    bias = ada_b.reshape(L, 1, N)
    return pl.pallas_call(
        _adaln_kernel,
        grid=(L, N // tn),
        in_specs=[
            pl.BlockSpec((D, B), lambda l, j: (0, 0)),
            pl.BlockSpec((None, D, tn), lambda l, j: (l, 0, j)),
            pl.BlockSpec((None, 1, tn), lambda l, j: (l, 0, j)),
        ],
        out_specs=pl.BlockSpec((None, B, tn), lambda l, j: (l, 0, j)),
        out_shape=jax.ShapeDtypeStruct((L, B, N), f32),
        compiler_params=_cparams("parallel", "parallel"),
        name="adaln",
    )(c.T, w, bias)


def _mod_spec(D):
    return lambda l: pl.BlockSpec((None, None, 3, D), lambda b, i: (l, b, 0, 0))


def _s5_in_kernel(x_ref, mod_ref, g_ref, w_ref, u_ref):
    g, shift, scale = g_ref[...], mod_ref[0:1, :], mod_ref[1:2, :]
    for rows in _subtiles(x_ref.shape[0]):
        hn = _modulate(x_ref[rows, :], g, shift, scale)
        u = _dot(hn.astype(bf16), w_ref[...])
        for k in range(u_ref.shape[0]):
            u_ref[k, rows, :] = u[:, k * LANES:(k + 1) * LANES]


def _s5_in(x, mod4, l, g, w_bf, tm):
    B, S, D = x.shape
    return pl.pallas_call(
        _s5_in_kernel,
        grid=(B, S // tm),
        in_specs=[
            pl.BlockSpec((None, tm, D), lambda b, i: (b, i, 0)),
            _mod_spec(D)(l),
            _resident((1, D)),
            _resident((D, D)),
        ],
        out_specs=pl.BlockSpec((None, D // LANES, tm, LANES), lambda b, i: (b, 0, i, 0)),
        out_shape=jax.ShapeDtypeStruct((B, D // LANES, S, LANES), f32),
        compiler_params=_cparams("parallel", "parallel"),
        name="s5_in",
    )(x, mod4, g, w_bf)


def _s5_weights_kernel(lre_ref, lim_ref, ldt_ref, br_ref, bi_ref, cr_ref, ci_ref,
                       wcs_ref, wout_ref, wtz_ref, adec_ref, *, lc, cg, p_state):
    lre = lre_ref[...]
    lim = lim_ref[...]
    dt = jnp.exp(ldt_ref[...])
    nl = lre.shape[1]
    mag = jnp.exp(lre * dt)
    lam_r = mag * jnp.cos(lim * dt)
    lam_i = mag * jnp.sin(lim * dt)
    num_r = lam_r - 1.0
    den = lre * lre + lim * lim
    k_r = (num_r * lre + lam_i * lim) / den
    k_i = (lam_i * lre - num_r * lim) / den
    br = br_ref[...]
    bi = bi_ref[...]
    bb_r = k_r * br - k_i * bi
    bb_i = k_r * bi + k_i * br
    cr = cr_ref[...]
    ci = ci_ref[...]

    ngrp = nl // p_state
    rows = ngrp * cg
    row_grp = lax.broadcasted_iota(jnp.int32, (rows, nl), 0) // cg
    lane_grp = lax.broadcasted_iota(jnp.int32, (rows, nl), 1) // p_state
    own = row_grp == lane_grp

    def slab(xr, xi):
        tr = jnp.where(own, jnp.tile(xr, (ngrp, 1)), 0.0)
        ti = jnp.where(own, jnp.tile(xi, (ngrp, 1)), 0.0)
        return jnp.concatenate([tr, ti], axis=1)

    pw = [(jnp.ones_like(lam_r), jnp.zeros_like(lam_r))]
    for _ in range(lc):
        pr, pi = pw[-1]
        pw.append((pr * lam_r - pi * lam_i, pr * lam_i + pi * lam_r))
    adec_ref[0:1, :] = pw[lc][0]
    adec_ref[1:2, :] = pw[lc][1]

    for s in range(lc):
        pr, pi = pw[lc - 1 - s]
        wcs_ref[s * rows:(s + 1) * rows, :] = slab(pr * bb_r - pi * bb_i,
                                                   pr * bb_i + pi * bb_r).astype(bf16)

    wq = []
    for t in range(lc + 1):
        pr, pi = pw[t]
        wq.append(slab(cr * pr - ci * pi, -(cr * pi + ci * pr)))
    for r in range(lc):
        wout_ref[:, r * rows:(r + 1) * rows] = wq[r + 1].T.astype(bf16)

    bsl = _split3(slab(bb_r, bb_i))
    kt = []
    for t in range(lc):
        q = _split3(wq[t])
        acc = _dot_nt(bsl[0], q[0])
        acc += _dot_nt(bsl[0], q[1]) + _dot_nt(bsl[1], q[0])
        acc += _dot_nt(bsl[1], q[1]) + _dot_nt(bsl[0], q[2]) + _dot_nt(bsl[2], q[0])
        kt.append(acc)
    zero = jnp.zeros((rows, rows), f32)
    for s in range(lc):
        for r in range(lc):
            blk = kt[r - s] if r >= s else zero
            wtz_ref[s * rows:(s + 1) * rows, r * rows:(r + 1) * rows] = blk.astype(bf16)


def _s5_weights(lam_re, lam_im, log_dt, b_re, b_im, c_re, c_im, lc):
    G, P = lam_re.shape
    Cg = b_re.shape[-1]
    gb = LANES // Cg
    nb = G // gb
    nl = gb * P
    rows = gb * Cg
    lre = lam_re.reshape(nb, 1, nl)
    lim = lam_im.reshape(nb, 1, nl)
    ldt = jnp.repeat(log_dt, P).reshape(nb, 1, nl)
    brt = b_re.reshape(nb, gb, P, Cg).transpose(0, 3, 1, 2).reshape(nb, Cg, nl)
    bit = b_im.reshape(nb, gb, P, Cg).transpose(0, 3, 1, 2).reshape(nb, Cg, nl)
    crt = c_re.reshape(nb, gb, Cg, P).transpose(0, 2, 1, 3).reshape(nb, Cg, nl)
    cit = c_im.reshape(nb, gb, Cg, P).transpose(0, 2, 1, 3).reshape(nb, Cg, nl)
    vec = pl.BlockSpec((None, 1, nl), lambda k: (k, 0, 0))
    mat = pl.BlockSpec((None, Cg, nl), lambda k: (k, 0, 0))
    return pl.pallas_call(
        functools.partial(_s5_weights_kernel, lc=lc, cg=Cg, p_state=P),
        grid=(nb,),
        in_specs=[vec, vec, vec, mat, mat, mat, mat],
        out_specs=[
            pl.BlockSpec((None, lc * rows, 2 * nl), lambda k: (k, 0, 0)),
            pl.BlockSpec((None, 2 * nl, lc * rows), lambda k: (k, 0, 0)),
            pl.BlockSpec((None, lc * rows, lc * rows), lambda k: (k, 0, 0)),
            pl.BlockSpec((None, 2, nl), lambda k: (k, 0, 0)),
        ],
        out_shape=[
            jax.ShapeDtypeStruct((nb, lc * rows, 2 * nl), bf16),
            jax.ShapeDtypeStruct((nb, 2 * nl, lc * rows), bf16),
            jax.ShapeDtypeStruct((nb, lc * rows, lc * rows), bf16),
            jax.ShapeDtypeStruct((nb, 2, nl), f32),
        ],
        compiler_params=_cparams("parallel"),
        name="s5_weights",
    )(lre, lim, ldt, brt, bit, crt, cit)


def _s5_core_kernel(u_ref, wcs_ref, wout_ref, wtz_ref, adec_ref, d_ref, z_ref,
                    st_ref, y1_ref, y_ref, *, lc, nseg, seg_len):
    nsets, nslab = st_ref.shape[0], st_ref.shape[1]
    half = nslab // 2
    seg_tok = seg_len * lc
    sl = seg_len // nsets

    mmg = min(S5_MM_SEGMENTS, nseg)

    def phase_a(seg0):
        xs = []
        for seg in range(seg0, seg0 + mmg):
            tok0 = seg * seg_tok
            xs.append(jnp.concatenate(
                [u_ref[pl.ds(tok0 + s, seg_len, stride=lc), :] for s in range(lc)], axis=1))
        xb = jnp.concatenate(xs, axis=0).astype(bf16)
        e = _dot(xb, wcs_ref[...])
        for g, seg in enumerate(range(seg0, seg0 + mmg)):
            for h in range(nsets):
                r0 = g * seg_len + h * sl
                for j in range(nslab):
                    st_ref[h, j, pl.ds(seg, sl, stride=nseg), :] = \
                        e[r0:r0 + sl, j * LANES:(j + 1) * LANES]
        y1_ref[pl.ds(seg0 * seg_len, mmg * seg_len), :] = _dot(xb, wtz_ref[...])

    for seg0 in range(0, nseg, mmg):
        phase_a(seg0)

    a_r = [jnp.broadcast_to(adec_ref[0:1, j * LANES:(j + 1) * LANES], (nseg, LANES))
           for j in range(half)]
    a_i = [jnp.broadcast_to(adec_ref[1:2, j * LANES:(j + 1) * LANES], (nseg, LANES))
           for j in range(half)]

    def step(i, hs, store):
        out = []
        rows = pl.ds(pl.multiple_of(i * nseg, nseg), nseg)
        for h in range(nsets):
            for j in range(half):
                hr, hi = hs[h * nslab + 2 * j], hs[h * nslab + 2 * j + 1]
                er = st_ref[h, j, rows, :]
                ei = st_ref[h, half + j, rows, :]
                if store:
                    st_ref[h, j, rows, :] = hr
                    st_ref[h, half + j, rows, :] = hi
                out.append(a_r[j] * hr - a_i[j] * hi + er)
                out.append(a_r[j] * hi + a_i[j] * hr + ei)
        return tuple(out)

    zeros = tuple(jnp.zeros((nseg, LANES), f32) for _ in range(nsets * nslab))
    ends = lax.fori_loop(0, sl, lambda i, hs: step(i, hs, False), zeros, unroll=S5_SCAN_UNROLL)

    starts = [None] * (nsets * nslab)
    for j in range(half):
        pr, pi = a_r[j][0:1, :], a_i[j][0:1, :]
        n = sl
        while n > 1:
            pr, pi = pr * pr - pi * pi, 2.0 * pr * pi
            n //= 2
        cr, ci = jnp.zeros((1, LANES), f32), jnp.zeros((1, LANES), f32)
        sr = [[] for _ in range(nsets)]
        si = [[] for _ in range(nsets)]
        for k in range(nseg):
            for h in range(nsets):
                sr[h].append(cr)
                si[h].append(ci)
                er = ends[h * nslab + 2 * j][k:k + 1, :]
                ei = ends[h * nslab + 2 * j + 1][k:k + 1, :]
                cr, ci = pr * cr - pi * ci + er, pr * ci + pi * cr + ei
        for h in range(nsets):
            starts[h * nslab + 2 * j] = jnp.concatenate(sr[h], axis=0)
            starts[h * nslab + 2 * j + 1] = jnp.concatenate(si[h], axis=0)
    lax.fori_loop(0, sl, lambda i, hs: step(i, hs, True), tuple(starts), unroll=S5_SCAN_UNROLL)

    d = d_ref[...]

    def phase_c(seg0):
        h = jnp.concatenate(
            [jnp.concatenate([st_ref[hh, j, pl.ds(seg, sl, stride=nseg), :]
                              for seg in range(seg0, seg0 + mmg) for hh in range(nsets)], axis=0)
             for j in range(nslab)], axis=1)
        y = _dot(h.astype(bf16), wout_ref[...])
        y = y + y1_ref[pl.ds(seg0 * seg_len, mmg * seg_len), :]
        for g, seg in enumerate(range(seg0, seg0 + mmg)):
            tok0 = seg * seg_tok
            for r in range(lc):
                y_ref[pl.ds(tok0 + r, seg_len, stride=lc), :] = \
                    y[g * seg_len:(g + 1) * seg_len, r * LANES:(r + 1) * LANES]

    def phase_d(seg0):
        rows = pl.ds(seg0 * seg_tok, mmg * seg_tok)
        z_ref[rows, :] = _gelu_tanh(y_ref[rows, :] + d * u_ref[rows, :]).astype(z_ref.dtype)

    groups = list(range(0, nseg, mmg))
    phase_c(groups[0])
    for gi, seg0 in enumerate(groups):
        if gi + 1 < len(groups):
            phase_c(groups[gi + 1])
        phase_d(seg0)


def _s5_core(u, wcs, wout, wtz, adec, d_skip, lc):
    B, nb, S, _ = u.shape
    nseg = S5_SEGMENTS
    m = S // lc
    seg_len = m // nseg
    assert seg_len * nseg * lc == S and seg_len & (seg_len - 1) == 0
    nslab = wcs.shape[-1] // LANES
    kw = wcs.shape[1]
    kern = functools.partial(_s5_core_kernel, lc=lc, nseg=nseg, seg_len=seg_len)
    return pl.pallas_call(
        kern,
        grid=(nb, B),
        in_specs=[
            pl.BlockSpec((None, None, S, LANES), lambda k, b: (b, k, 0, 0)),
            pl.BlockSpec((None, kw, nslab * LANES), lambda k, b: (k, 0, 0)),
            pl.BlockSpec((None, nslab * LANES, kw), lambda k, b: (k, 0, 0)),
            pl.BlockSpec((None, kw, kw), lambda k, b: (k, 0, 0)),
            pl.BlockSpec((None, 2, adec.shape[-1]), lambda k, b: (k, 0, 0)),
            pl.BlockSpec((1, LANES), lambda k, b: (0, k)),
        ],
        out_specs=pl.BlockSpec((None, None, S, LANES), lambda k, b: (b, k, 0, 0)),
        out_shape=jax.ShapeDtypeStruct((B, nb, S, LANES), bf16),
        scratch_shapes=[
            pltpu.VMEM((S5_SCAN_SETS, nslab, nseg * seg_len // S5_SCAN_SETS, LANES), f32),
            pltpu.VMEM((m, kw), f32),
            pltpu.VMEM((S, LANES), f32),
        ],
        compiler_params=_cparams("parallel", "parallel"),
        name="s5_core",
    )(u, wcs, wout, wtz, adec, d_skip)


def _s5_out_kernel(z_ref, x_ref, mod_ref, wg_ref, wo_ref, h_ref):
    subs = _subtiles(x_ref.shape[0])
    gate = mod_ref[2:3, :]

    def z_rows(rows):
        return jnp.concatenate([z_ref[k, rows, :] for k in range(z_ref.shape[0])], axis=1)

    def glu_logits(rows):
        return _dot(z_rows(rows), wg_ref[...])

    g_next = glu_logits(subs[0])
    for k, rows in enumerate(subs):
        g = g_next
        if k + 1 < len(subs):
            g_next = glu_logits(subs[k + 1])
        z2 = z_rows(rows).astype(f32) * (1.0 / (1.0 + jnp.exp(-g)))
        m = _dot(z2.astype(bf16), wo_ref[...])
        h_ref[rows, :] = x_ref[rows, :] + gate * m


def _s5_out(z, x, mod4, l, wg_bf, wo_bf, tm):
    B, S, D = x.shape
    row = pl.BlockSpec((None, tm, D), lambda b, i: (b, i, 0))
    zblk = pl.BlockSpec((None, D // LANES, tm, LANES), lambda b, i: (b, 0, i, 0))
    return pl.pallas_call(
        _s5_out_kernel,
        grid=(B, S // tm),
        in_specs=[zblk, row, _mod_spec(D)(l), _resident((D, D)), _resident((D, D))],
        out_specs=row,
        out_shape=jax.ShapeDtypeStruct((B, S, D), f32),
        compiler_params=_cparams("parallel", "parallel"),
        name="s5_out",
    )(z, x, mod4, wg_bf, wo_bf)


def _ffn_kernel(h_ref, halo_ref, mod_ref, g_ref, wu_ref, wd_ref, cw_ref, cb_ref,
                fg_ref, o_ref, hn_ref, a_ref, b_ref, gt_ref, *, fc, final_norm):
    i = pl.program_id(1)
    tm = h_ref.shape[0]
    hb = halo_ref.shape[0]
    taps, F = cw_ref.shape
    nc = F // fc
    g, shift, scale = g_ref[...], mod_ref[0:1, :], mod_ref[1:2, :]
    x = h_ref[...]
    keep = (i > 0).astype(f32)
    hn_ref[0:hb, :] = (_modulate(halo_ref[...], g, shift, scale) * keep).astype(bf16)
    a_ref[0, 0:hb, :] = _dot(hn_ref[0:hb, :], wu_ref[:, 0:fc])
    for rows in _subtiles(tm):
        hrows = slice(hb + rows.start, hb + rows.stop)
        hn = _modulate(h_ref[rows, :], g, shift, scale).astype(bf16)
        hn_ref[hrows, :] = hn
        a_ref[0, hrows, :] = _dot(hn, wu_ref[:, 0:fc])
        b_ref[0, rows, :] = _dot(hn, wu_ref[:, F:F + fc])

    def up(j, slot):
        cols = slice(j * fc, (j + 1) * fc)
        a_ref[slot] = _dot(hn_ref[...], wu_ref[:, cols])
        b_ref[slot] = _dot(hn_ref[hb:, :], wu_ref[:, F + j * fc:F + (j + 1) * fc])

    def gate(j, slot):
        cols = slice(j * fc, (j + 1) * fc)
        cw = cw_ref[:, cols]
        ac = cb_ref[:, cols]
        for t in range(taps):
            off = hb - (taps - 1) + t
            ac = ac + cw[t:t + 1, :] * a_ref[slot, off:off + tm, :]
        gt_ref[:, cols] = (_silu(ac) * b_ref[slot]).astype(bf16)

    for j in range(nc):
        if j + 1 < nc:
            up(j + 1, (j + 1) % 2)
        gate(j, j % 2)
    out = x + mod_ref[2:3, :] * _dot(gt_ref[...], wd_ref[...])
    if final_norm:
        ms = jnp.mean(out * out, axis=-1, keepdims=True)
        out = out * lax.rsqrt(ms + EPS) * fg_ref[...]
    o_ref[...] = out


def _ffn(h, mod4, l, g, w_up, conv_w, conv_b, w_down, final_g, tm, fc, final_norm):
    B, S, D = h.shape
    F = w_down.shape[0]
    taps = conv_w.shape[0]
    assert F % fc == 0 and F % LANES == 0
    hb = BF16_ROWS
    per = tm // hb
    kern = functools.partial(_ffn_kernel, fc=fc, final_norm=final_norm)
    return pl.pallas_call(
        kern,
        grid=(B, S // tm),
        in_specs=[
            pl.BlockSpec((None, tm, D), lambda b, i: (b, i, 0)),
            pl.BlockSpec((None, hb, D), lambda b, i: (b, jnp.maximum(i * per - 1, 0), 0)),
            _mod_spec(D)(l),
            _resident((1, D)),
            _resident((D, 2 * F)),
            _resident((F, D)),
            _resident((taps, F)),
            _resident((1, F)),
            _resident((1, D)),
        ],
        out_specs=pl.BlockSpec((None, tm, D), lambda b, i: (b, i, 0)),
        out_shape=jax.ShapeDtypeStruct((B, S, D), f32),
        scratch_shapes=[
            pltpu.VMEM((hb + tm, D), bf16),
            pltpu.VMEM((2, hb + tm, fc), f32),
            pltpu.VMEM((2, tm, fc), f32),
            pltpu.VMEM((tm, F), bf16),
        ],
        compiler_params=_cparams("parallel", "parallel"),
        name="ffn",
    )(h, h, mod4, g, w_up.astype(bf16), w_down.astype(bf16), conv_w, conv_b.reshape(1, F),
      final_g)


LOG2E = 1.4426950408889634


FBIAS_PIECES = 3


def _fbias_selector(n_heads):
    heads = 2
    sel = np.zeros((FBIAS_PIECES * LANES, (n_heads // heads) * LANES), np.float32)
    for h in range(n_heads):
        for p in range(FBIAS_PIECES):
            sel[p * LANES + h, (h // heads) * LANES + (h % heads) * FBIAS_PIECES + p] = 1.0
    return jnp.asarray(sel, dtype=bf16)


def _fox_in_kernel(h_ref, mod_ref, g_ref, wqt_ref, wk_ref, wvt_ref, wf_ref, bf_ref, sel_ref,
                   qt_ref, k_ref, vt_ref, fa_ref, carry_ref, *, q_scale, ts):
    i = pl.program_id(1)
    tm, D = h_ref.shape

    @pl.when(i == 0)
    def _():
        carry_ref[...] = jnp.zeros_like(carry_ref)

    g, shift, scale = g_ref[...], mod_ref[0:1, :], mod_ref[1:2, :]
    subs = _subtiles(tm)
    sub = subs[0].stop
    r = lax.broadcasted_iota(jnp.int32, (sub, sub), 0)
    c = lax.broadcasted_iota(jnp.int32, (sub, sub), 1)
    tri = (c <= r).astype(bf16)
    carry = carry_ref[...]
    for rows in subs:
        hn = _modulate(h_ref[rows, :], g, shift, scale).astype(bf16)
        k_ref[rows, :] = _dot(hn, wk_ref[...]).astype(bf16)
        qt_ref[:, rows] = (_dot_nt(wqt_ref[...], hn) * q_scale).astype(bf16)
        vt = _dot_nt(wvt_ref[...], hn).astype(bf16)
        for cc in range(sub // ts):
            vt_ref[rows.start // ts + cc] = vt[:, cc * ts:(cc + 1) * ts]

        fl = _dot(hn, wf_ref[...]) + bf_ref[...]
        lf = jnp.minimum(fl, 0.0) - jnp.log(1.0 + jnp.exp(-jnp.abs(fl)))
        p0, p1, p2 = _split3(lf)
        cs = _dot(tri, p0) + _dot(tri, p1) + _dot(tri, p2) + carry
        carry = cs[sub - 1:sub, :]
        fa = _dot(jnp.concatenate(_split3(cs * LOG2E), axis=1), sel_ref[...]).astype(bf16)
        for hp in range(fa_ref.shape[0]):
            fa_ref[hp, rows, :] = fa[:, hp * LANES:(hp + 1) * LANES]
    carry_ref[...] = carry


def _fox_in(h, mod4, l, g, wqt_bf, wk_bf, wvt_bf, wf_bf, bf_pad, sel, q_scale, tm, ts):
    B, S, D = h.shape
    nhp = sel.shape[1] // LANES
    return pl.pallas_call(
        functools.partial(_fox_in_kernel, q_scale=q_scale, ts=ts),
        grid=(B, S // tm),
        in_specs=[
            pl.BlockSpec((None, tm, D), lambda b, i: (b, i, 0)),
            _mod_spec(D)(l),
            _resident((1, D)),
            _resident((D, D)),
            _resident((D, D)),
            _resident((D, D)),
            _resident((D, LANES)),
            _resident((1, LANES)),
            _resident(sel.shape),
        ],
        out_specs=[
            pl.BlockSpec((None, D, tm), lambda b, i: (b, 0, i)),
            pl.BlockSpec((None, tm, D), lambda b, i: (b, i, 0)),
            pl.BlockSpec((None, tm // ts, D, ts), lambda b, i: (b, i, 0, 0)),
            pl.BlockSpec((None, nhp, tm, LANES), lambda b, i: (b, 0, i, 0)),
        ],
        out_shape=[
            jax.ShapeDtypeStruct((B, D, S), bf16),
            jax.ShapeDtypeStruct((B, S, D), bf16),
            jax.ShapeDtypeStruct((B, S // ts, D, ts), bf16),
            jax.ShapeDtypeStruct((B, nhp, S, LANES), bf16),
        ],
        scratch_shapes=[pltpu.VMEM((1, LANES), f32)],
        compiler_params=_cparams("parallel", "arbitrary"),
        name="fox_in",
    )(h, mod4, g, wqt_bf, wk_bf, wvt_bf, wf_bf, bf_pad, sel)


NEG_BIG = -1e30
ATTN_PAIR = 2
ATTN_GROUPS_PER_TRIP = 4


def _sublane_allreduce(x, op):
    for sh in (4, 2, 1):
        x = op(x, pltpu.roll(x, sh, axis=0))
    return x


def _tree(xs, op):
    while len(xs) > 1:
        xs = [op(xs[i], xs[i + 1]) if i + 1 < len(xs) else xs[i] for i in range(0, len(xs), 2)]
    return xs[0]


def _attn_kernel(qta_ref, qtb_ref, k_ref, fa_ref, vt_ref, o_ref, s_ref, mx_ref, *, ts, dh):
    i = pl.program_id(2)
    nq = 2 * pl.num_programs(2)
    _attn_block(i, qta_ref, k_ref, fa_ref, vt_ref, o_ref.at[0], s_ref, mx_ref, ts=ts, dh=dh)
    _attn_block(nq - 1 - i, qtb_ref, k_ref, fa_ref, vt_ref, o_ref.at[1], s_ref, mx_ref,
                ts=ts, dh=dh)


def _attn_block(qi, qt_ref, k_ref, fa_ref, vt_ref, o_ref, s_ref, mx_ref, *, ts, dh):
    tq = qt_ref.shape[1]
    heads = LANES // dh
    spq = tq // ts

    qt = qt_ref[...]
    rowi = lax.broadcasted_iota(jnp.int32, qt.shape, 0)
    qts = []
    for e in range(heads):
        own_f = (rowi // FBIAS_PIECES) == e
        qts.append(jnp.concatenate(
            [jnp.where((rowi // dh) == e, qt, jnp.zeros_like(qt)),
             jnp.where(own_f, -1.0, 0.0).astype(bf16)], axis=0))
    nrt = ts // SUBLANES

    def issue(g, bank):
        gk = spq * ts
        ks = pl.ds(pl.multiple_of(g * gk, gk), gk)
        kb = jnp.concatenate([k_ref[ks, :], fa_ref[ks, :]], axis=1)
        for e in range(heads):
            s = _dot(kb, qts[e])
            for d in range(spq):
                sd = s[d * ts:(d + 1) * ts, :]
                s_ref[bank, d, e] = sd
                mx_ref[bank, d, e] = _tree([sd[r * SUBLANES:(r + 1) * SUBLANES, :]
                                            for r in range(nrt)], jnp.maximum)

    ones_rows = (lax.broadcasted_iota(jnp.int32, (BF16_ROWS, ATTN_PAIR * ts), 0) == 0).astype(bf16)
    acc_rows = dh + SUBLANES

    def absorb(g, bank, ds, carry, masked):
        vtb = jnp.concatenate([vt_ref[g * spq + d] for d in ds], axis=1)
        lane0 = ds[0] * ts if masked else 0
        out = []
        for e in range(heads):
            m_all, acc_all = carry[2 * e:2 * e + 2]
            m, acc = m_all[:, lane0:], acc_all[:, lane0:]
            if not masked:
                tiles = None
                mx = _tree([mx_ref[bank, d, e] for d in ds], jnp.maximum)
            else:
                tiles = []
                for d in ds:
                    kidx = lax.broadcasted_iota(jnp.int32, (ts, tq - lane0), 0) + d * ts
                    qidx = lax.broadcasted_iota(jnp.int32, (ts, tq - lane0), 1) + lane0
                    sm = jnp.where(kidx <= qidx, s_ref[bank, d, e, :, lane0:], NEG_BIG)
                    tiles += [sm[r * SUBLANES:(r + 1) * SUBLANES, :] for r in range(nrt)]
                mx = _tree(tiles, jnp.maximum)
            m_new = jnp.maximum(m, _sublane_allreduce(mx, jnp.maximum))
            alpha = jnp.exp2(m - m_new)
            if tiles is None:
                tiles = [s_ref[bank, d, e, r * SUBLANES:(r + 1) * SUBLANES, :]
                         for d in ds for r in range(nrt)]
            p = jnp.concatenate([jnp.exp2(t - m_new) for t in tiles], axis=0).astype(bf16)
            lhs = jnp.concatenate([vtb[e * dh:(e + 1) * dh, :], ones_rows], axis=0)
            pv = _dot(lhs, p)[:acc_rows, :]
            acc = jnp.concatenate([alpha] * (acc_rows // SUBLANES), axis=0) * acc + pv
            if lane0:
                m_new = jnp.concatenate([m_all[:, :lane0], m_new], axis=1)
                acc = jnp.concatenate([acc_all[:, :lane0], acc], axis=1)
            out += [m_new, acc]
        return tuple(out)

    assert spq % ATTN_PAIR == 0

    def group(g, bank, carry, masked):
        for d0 in range(0, spq, ATTN_PAIR):
            carry = absorb(g, bank, list(range(d0, d0 + ATTN_PAIR)), carry, masked)
        return carry

    init = []
    for e in range(heads):
        init += [jnp.full((SUBLANES, tq), NEG_BIG, f32), jnp.zeros((acc_rows, tq), f32)]
    issue(0, 0)

    gpt = ATTN_GROUPS_PER_TRIP

    def trip(t, c):
        g0 = gpt * t
        for k in range(gpt):
            issue(g0 + k + 1, (k + 1) % 2)
            c = group(g0 + k, k % 2, c, False)
        return c

    carry = lax.fori_loop(0, qi // gpt, trip, tuple(init))
    g0 = qi - qi % gpt

    def tail(r):
        def run(c):
            for k in range(r):
                issue(g0 + k + 1, (k + 1) % 2)
                c = group(g0 + k, k % 2, c, False)
            return group(g0 + r, r % 2, c, True)
        return run

    carry = lax.switch(qi % gpt, [tail(r) for r in range(gpt)], carry)
    outs = []
    for e in range(heads):
        acc = carry[2 * e + 1]
        l = jnp.broadcast_to(acc[dh:dh + 1, :], (dh, tq))
        outs.append(acc[:dh, :] / l)
    o_ref[...] = jnp.concatenate(outs, axis=0).T.astype(o_ref.dtype)


def _attn(qt, k, fa, vt, n_heads, tq):
    B, D, S = qt.shape
    ts = vt.shape[-1]
    dh = D // n_heads
    heads = LANES // dh
    nhp = D // LANES
    assert heads * FBIAS_PIECES <= LANES and heads == 2
    nq = S // tq
    assert nq % 2 == 0
    kern = functools.partial(_attn_kernel, ts=ts, dh=dh)
    return pl.pallas_call(
        kern,
        grid=(B, nhp, nq // 2),
        in_specs=[
            pl.BlockSpec((None, LANES, tq), lambda b, h, i: (b, h, i)),
            pl.BlockSpec((None, LANES, tq), lambda b, h, i: (b, h, nq - 1 - i)),
            pl.BlockSpec((None, S, LANES), lambda b, h, i: (b, 0, h)),
            pl.BlockSpec((None, None, S, LANES), lambda b, h, i: (b, h, 0, 0)),
            pl.BlockSpec((None, S // ts, LANES, ts), lambda b, h, i: (b, 0, h, 0)),
        ],
        out_specs=pl.BlockSpec((None, None, 2, tq, LANES), lambda b, h, i: (b, i, 0, 0, h)),
        out_shape=jax.ShapeDtypeStruct((B, nq // 2, 2, tq, D), bf16),
        scratch_shapes=[pltpu.VMEM((2, tq // ts, heads, ts, tq), f32),
                        pltpu.VMEM((2, tq // ts, heads, SUBLANES, tq), f32)],
        compiler_params=_cparams("parallel", "parallel", "arbitrary"),
        name="attn",
    )(qt, qt, k, fa, vt)


def _fox_out_kernel(o_ref, h_ref, mod_ref, w_ref, out_ref):
    gate = mod_ref[2:3, :]
    for rows in _subtiles(o_ref.shape[0]):
        out_ref[rows, :] = h_ref[rows, :] + gate * _dot(o_ref[rows, :], w_ref[...])


def _fox_out(o, h, mod4, l, w_bf):
    B, S, D = h.shape
    tm = o.shape[3]
    nq = S // tm
    row = pl.BlockSpec((None, tm, D), lambda b, i: (b, i, 0))
    orow = pl.BlockSpec(
        (None, None, None, tm, D),
        lambda b, i: (b, jnp.where(i < nq // 2, i, nq - 1 - i), (i >= nq // 2).astype(jnp.int32),
                      0, 0))
    return pl.pallas_call(
        _fox_out_kernel,
        grid=(B, S // tm),
        in_specs=[orow, row, _mod_spec(D)(l), _resident((D, D))],
        out_specs=row,
        out_shape=jax.ShapeDtypeStruct((B, S, D), f32),
        compiler_params=_cparams("parallel", "parallel"),
        name="fox_out",
    )(o, h, mod4, w_bf)


def kernel(x, c, norm_g, ada_w, ada_b, s5_w_in, s5_lam_re, s5_lam_im, s5_log_dt, s5_b_re, s5_b_im, s5_c_re, s5_c_im, s5_d, s5_w_glu, s5_w_out, fox_w_in, fox_b_f, fox_w_out, ffn_w_up, ffn_conv_w, ffn_conv_b, ffn_w_down, final_g):
    B, S, D = x.shape
    depth = norm_g.shape[0]
    n_heads = fox_b_f.shape[-1]
    tm = min(1024, S)
    tmm = min(1024, S)
    fc = 256
    final = final_g.reshape(1, D)

    mod = _adaln(c, ada_w, ada_b)
    mod4 = mod.reshape(depth * 2, B, 3, D)

    h = x
    for i in range(depth):
        j = i // 2
        g_mix = norm_g[i, 0].reshape(1, D)
        g_ffn = norm_g[i, 1].reshape(1, D)
        if i % 2 == 0:
            u = _s5_in(h, mod4, 2 * i, g_mix, s5_w_in[j].astype(bf16), tmm)
            wcs, wout, wtz, adec = _s5_weights(s5_lam_re[j], s5_lam_im[j], s5_log_dt[j],
                                               s5_b_re[j], s5_b_im[j], s5_c_re[j], s5_c_im[j],
                                               S5_CHUNK)
            z = _s5_core(u, wcs, wout, wtz, adec, s5_d[j].reshape(1, D), S5_CHUNK)
            h = _s5_out(z, h, mod4, 2 * i, s5_w_glu[j].astype(bf16),
                        s5_w_out[j].astype(bf16), tmm)
        else:
            w_in = fox_w_in[j]
            wqt = w_in[:, :D].T.astype(bf16)
            wk = w_in[:, D:2 * D].astype(bf16)
            wvt = w_in[:, 2 * D:3 * D].T.astype(bf16)
            wf = jnp.pad(w_in[:, 3 * D:], ((0, 0), (0, LANES - n_heads))).astype(bf16)
            bfp = jnp.pad(fox_b_f[j], (0, LANES - n_heads)).reshape(1, LANES)
            ts = LANES
            q_scale = float((D // n_heads) ** -0.5 * LOG2E)
            qt, k, vt, fa = _fox_in(h, mod4, 2 * i, g_mix, wqt, wk, wvt, wf, bfp,
                                    _fbias_selector(n_heads), q_scale, tmm, ts)
            o = _attn(qt, k, fa, vt, n_heads, tq=min(512, S))
            h = _fox_out(o, h, mod4, 2 * i, fox_w_out[j].astype(bf16))
        h = _ffn(h, mod4, 2 * i + 1, g_ffn, ffn_w_up[i], ffn_conv_w[i], ffn_conv_b[i],
                 ffn_w_down[i], final, tm, fc, final_norm=(i == depth - 1))
    return h
```

```python
import functools

import jax
import jax.numpy as jnp
import numpy as np
from jax import lax
from jax.experimental import pallas as pl
from jax.experimental.pallas import tpu as pltpu

EPS = 1e-6
LANES = 128
SUBLANES = 8
BF16_ROWS = 16
VMEM_LIMIT = 56 * 1024 * 1024

S5_CHUNK = 4
S5_SEGMENTS = SUBLANES
S5_SCAN_UNROLL = 4
S5_SCAN_SETS = 2
S5_MM_SEGMENTS = 4

f32 = jnp.float32
bf16 = jnp.bfloat16


def _cparams(*sem):
    return pltpu.CompilerParams(dimension_semantics=sem, vmem_limit_bytes=VMEM_LIMIT)


def _resident(shape):
    nd = len(shape)
    return pl.BlockSpec(shape, lambda *_: (0,) * nd, pipeline_mode=pl.Buffered(1))


def _dot(a, b):
    return jnp.dot(a, b, preferred_element_type=f32)


def _dot_nt(a, b):
    return lax.dot_general(a, b, (((1,), (1,)), ((), ())), preferred_element_type=f32)


SUB_ROWS = 256


def _subtiles(tm):
    sub = min(SUB_ROWS, tm)
    assert tm % sub == 0
    return [slice(k * sub, (k + 1) * sub) for k in range(tm // sub)]


def _split3(x):
    hi = x.astype(bf16)
    r1 = x - hi.astype(f32)
    mid = r1.astype(bf16)
    lo = (r1 - mid.astype(f32)).astype(bf16)
    return hi, mid, lo


def _modulate(x, g, shift, scale):
    ms = jnp.mean(x * x, axis=-1, keepdims=True)
    return x * lax.rsqrt(ms + EPS) * g * (1.0 + scale) + shift


def _silu(x):
    return x * (1.0 / (1.0 + jnp.exp(-x)))


def _gelu_tanh(x):
    c = 0.7978845608028654
    return 0.5 * x * (1.0 + jnp.tanh(c * (x + 0.044715 * (x * x * x))))


def _adaln_kernel(ct_ref, w_ref, b_ref, o_ref):
    ct = ct_ref[...]
    cs = _silu(ct)
    D = ct.shape[0]
    rc = min(SUB_ROWS, D)
    for b in range(ct.shape[1]):
        acc = b_ref[...]
        for r0 in range(0, D, rc):
            acc = acc + jnp.sum(w_ref[r0:r0 + rc, :] * cs[r0:r0 + rc, b:b + 1], axis=0,
                                keepdims=True)
        o_ref[b:b + 1, :] = acc


def _adaln(c, ada_w, ada_b):
    B, D = c.shape
    L = ada_w.shape[0] * ada_w.shape[1]
    N = ada_w.shape[-1]
    tn = 1024 if N % 1024 == 0 else 512
    w = ada_w.reshape(L, D, N)
    bias = ada_b.reshape(L, 1, N)
    return pl.pallas_call(
        _adaln_kernel,
        grid=(L, N // tn),
        in_specs=[
            pl.BlockSpec((D, B), lambda l, j: (0, 0)),
            pl.BlockSpec((None, D, tn), lambda l, j: (l, 0, j)),
            pl.BlockSpec((None, 1, tn), lambda l, j: (l, 0, j)),
        ],
        out_specs=pl.BlockSpec((None, B, tn), lambda l, j: (l, 0, j)),
        out_shape=jax.ShapeDtypeStruct((L, B, N), f32),
        compiler_params=_cparams("parallel", "parallel"),
        name="adaln",
    )(c.T, w, bias)


def _mod_spec(D):
    return lambda l: pl.BlockSpec((None, None, 3, D), lambda b, i: (l, b, 0, 0))


def _s5_in_kernel(x_ref, mod_ref, g_ref, w_ref, u_ref):
    g, shift, scale = g_ref[...], mod_ref[0:1, :], mod_ref[1:2, :]
    for rows in _subtiles(x_ref.shape[0]):
        hn = _modulate(x_ref[rows, :], g, shift, scale)
        u = _dot(hn.astype(bf16), w_ref[...])
        for k in range(u_ref.shape[0]):
            u_ref[k, rows, :] = u[:, k * LANES:(k + 1) * LANES]


def _s5_in(x, mod4, l, g, w_bf, tm):
    B, S, D = x.shape
    return pl.pallas_call(
        _s5_in_kernel,
        grid=(B, S // tm),
        in_specs=[
            pl.BlockSpec((None, tm, D), lambda b, i: (b, i, 0)),
            _mod_spec(D)(l),
            _resident((1, D)),
            _resident((D, D)),
        ],
        out_specs=pl.BlockSpec((None, D // LANES, tm, LANES), lambda b, i: (b, 0, i, 0)),
        out_shape=jax.ShapeDtypeStruct((B, D // LANES, S, LANES), f32),
        compiler_params=_cparams("parallel", "parallel"),
        name="s5_in",
    )(x, mod4, g, w_bf)


def _s5_weights_kernel(lre_ref, lim_ref, ldt_ref, br_ref, bi_ref, cr_ref, ci_ref,
                       wcs_ref, wout_ref, wtz_ref, adec_ref, *, lc, cg, p_state):
    lre = lre_ref[...]
    lim = lim_ref[...]
    dt = jnp.exp(ldt_ref[...])
    nl = lre.shape[1]
    mag = jnp.exp(lre * dt)
    lam_r = mag * jnp.cos(lim * dt)
    lam_i = mag * jnp.sin(lim * dt)
    num_r = lam_r - 1.0
    den = lre * lre + lim * lim
    k_r = (num_r * lre + lam_i * lim) / den
    k_i = (lam_i * lre - num_r * lim) / den
    br = br_ref[...]
    bi = bi_ref[...]
    bb_r = k_r * br - k_i * bi
    bb_i = k_r * bi + k_i * br
    cr = cr_ref[...]
    ci = ci_ref[...]

    ngrp = nl // p_state
    rows = ngrp * cg
    row_grp = lax.broadcasted_iota(jnp.int32, (rows, nl), 0) // cg
    lane_grp = lax.broadcasted_iota(jnp.int32, (rows, nl), 1) // p_state
    own = row_grp == lane_grp

    def slab(xr, xi):
        tr = jnp.where(own, jnp.tile(xr, (ngrp, 1)), 0.0)
        ti = jnp.where(own, jnp.tile(xi, (ngrp, 1)), 0.0)
        return jnp.concatenate([tr, ti], axis=1)

    pw = [(jnp.ones_like(lam_r), jnp.zeros_like(lam_r))]
    for _ in range(lc):
        pr, pi = pw[-1]
        pw.append((pr * lam_r - pi * lam_i, pr * lam_i + pi * lam_r))
    adec_ref[0:1, :] = pw[lc][0]
    adec_ref[1:2, :] = pw[lc][1]

    for s in range(lc):
        pr, pi = pw[lc - 1 - s]
        wcs_ref[s * rows:(s + 1) * rows, :] = slab(pr * bb_r - pi * bb_i,
                                                   pr * bb_i + pi * bb_r).astype(bf16)

    wq = []
    for t in range(lc + 1):
        pr, pi = pw[t]
        wq.append(slab(cr * pr - ci * pi, -(cr * pi + ci * pr)))
    for r in range(lc):
        wout_ref[:, r * rows:(r + 1) * rows] = wq[r + 1].T.astype(bf16)

    bsl = _split3(slab(bb_r, bb_i))
    kt = []
    for t in range(lc):
        q = _split3(wq[t])
        acc = _dot_nt(bsl[0], q[0])
        acc += _dot_nt(bsl[0], q[1]) + _dot_nt(bsl[1], q[0])
        acc += _dot_nt(bsl[1], q[1]) + _dot_nt(bsl[0], q[2]) + _dot_nt(bsl[2], q[0])
        kt.append(acc)
    zero = jnp.zeros((rows, rows), f32)
    for s in range(lc):
        for r in range(lc):
            blk = kt[r - s] if r >= s else zero
            wtz_ref[s * rows:(s + 1) * rows, r * rows:(r + 1) * rows] = blk.astype(bf16)


def _s5_weights(lam_re, lam_im, log_dt, b_re, b_im, c_re, c_im, lc):
    G, P = lam_re.shape
    Cg = b_re.shape[-1]
    gb = LANES // Cg
    nb = G // gb
    nl = gb * P
    rows = gb * Cg
    lre = lam_re.reshape(nb, 1, nl)
    lim = lam_im.reshape(nb, 1, nl)
    ldt = jnp.repeat(log_dt, P).reshape(nb, 1, nl)
    brt = b_re.reshape(nb, gb, P, Cg).transpose(0, 3, 1, 2).reshape(nb, Cg, nl)
    bit = b_im.reshape(nb, gb, P, Cg).transpose(0, 3, 1, 2).reshape(nb, Cg, nl)
    crt = c_re.reshape(nb, gb, Cg, P).transpose(0, 2, 1, 3).reshape(nb, Cg, nl)
    cit = c_im.reshape(nb, gb, Cg, P).transpose(0, 2, 1, 3).reshape(nb, Cg, nl)
    vec = pl.BlockSpec((None, 1, nl), lambda k: (k, 0, 0))
    mat = pl.BlockSpec((None, Cg, nl), lambda k: (k, 0, 0))
    return pl.pallas_call(
        functools.partial(_s5_weights_kernel, lc=lc, cg=Cg, p_state=P),
        grid=(nb,),
        in_specs=[vec, vec, vec, mat, mat, mat, mat],
        out_specs=[
            pl.BlockSpec((None, lc * rows, 2 * nl), lambda k: (k, 0, 0)),
            pl.BlockSpec((None, 2 * nl, lc * rows), lambda k: (k, 0, 0)),
            pl.BlockSpec((None, lc * rows, lc * rows), lambda k: (k, 0, 0)),
            pl.BlockSpec((None, 2, nl), lambda k: (k, 0, 0)),
        ],
        out_shape=[
            jax.ShapeDtypeStruct((nb, lc * rows, 2 * nl), bf16),
            jax.ShapeDtypeStruct((nb, 2 * nl, lc * rows), bf16),
            jax.ShapeDtypeStruct((nb, lc * rows, lc * rows), bf16),
            jax.ShapeDtypeStruct((nb, 2, nl), f32),
        ],
        compiler_params=_cparams("parallel"),
        name="s5_weights",
    )(lre, lim, ldt, brt, bit, crt, cit)


def _s5_core_kernel(u_ref, wcs_ref, wout_ref, wtz_ref, adec_ref, d_ref, z_ref,
                    st_ref, y1_ref, y_ref, *, lc, nseg, seg_len):
    nsets, nslab = st_ref.shape[0], st_ref.shape[1]
    half = nslab // 2
    seg_tok = seg_len * lc
    sl = seg_len // nsets

    mmg = min(S5_MM_SEGMENTS, nseg)

    def phase_a(seg0):
        xs = []
        for seg in range(seg0, seg0 + mmg):
            tok0 = seg * seg_tok
            xs.append(jnp.concatenate(
                [u_ref[pl.ds(tok0 + s, seg_len, stride=lc), :] for s in range(lc)], axis=1))
        xb = jnp.concatenate(xs, axis=0).astype(bf16)
        e = _dot(xb, wcs_ref[...])
        for g, seg in enumerate(range(seg0, seg0 + mmg)):
            for h in range(nsets):
                r0 = g * seg_len + h * sl
                for j in range(nslab):
                    st_ref[h, j, pl.ds(seg, sl, stride=nseg), :] = \
                        e[r0:r0 + sl, j * LANES:(j + 1) * LANES]
        y1_ref[pl.ds(seg0 * seg_len, mmg * seg_len), :] = _dot(xb, wtz_ref[...])

    for seg0 in range(0, nseg, mmg):
        phase_a(seg0)

    a_r = [jnp.broadcast_to(adec_ref[0:1, j * LANES:(j + 1) * LANES], (nseg, LANES))
           for j in range(half)]
    a_i = [jnp.broadcast_to(adec_ref[1:2, j * LANES:(j + 1) * LANES], (nseg, LANES))
           for j in range(half)]

    def step(i, hs, store):
        out = []
        rows = pl.ds(pl.multiple_of(i * nseg, nseg), nseg)
        for h in range(nsets):
            for j in range(half):
                hr, hi = hs[h * nslab + 2 * j], hs[h * nslab + 2 * j + 1]
                er = st_ref[h, j, rows, :]
                ei = st_ref[h, half + j, rows, :]
                if store:
                    st_ref[h, j, rows, :] = hr
                    st_ref[h, half + j, rows, :] = hi
                out.append(a_r[j] * hr - a_i[j] * hi + er)
                out.append(a_r[j] * hi + a_i[j] * hr + ei)
        return tuple(out)

    zeros = tuple(jnp.zeros((nseg, LANES), f32) for _ in range(nsets * nslab))
    ends = lax.fori_loop(0, sl, lambda i, hs: step(i, hs, False), zeros, unroll=S5_SCAN_UNROLL)

    starts = [None] * (nsets * nslab)
    for j in range(half):
        pr, pi = a_r[j][0:1, :], a_i[j][0:1, :]
        n = sl
        while n > 1:
            pr, pi = pr * pr - pi * pi, 2.0 * pr * pi
            n //= 2
        cr, ci = jnp.zeros((1, LANES), f32), jnp.zeros((1, LANES), f32)
        sr = [[] for _ in range(nsets)]
        si = [[] for _ in range(nsets)]
        for k in range(nseg):
            for h in range(nsets):
                sr[h].append(cr)
                si[h].append(ci)
                er = ends[h * nslab + 2 * j][k:k + 1, :]
                ei = ends[h * nslab + 2 * j + 1][k:k + 1, :]
                cr, ci = pr * cr - pi * ci + er, pr * ci + pi * cr + ei
        for h in range(nsets):
            starts[h * nslab + 2 * j] = jnp.concatenate(sr[h], axis=0)
            starts[h * nslab + 2 * j + 1] = jnp.concatenate(si[h], axis=0)
    lax.fori_loop(0, sl, lambda i, hs: step(i, hs, True), tuple(starts), unroll=S5_SCAN_UNROLL)

    d = d_ref[...]

    def phase_c(seg0):
        h = jnp.concatenate(
            [jnp.concatenate([st_ref[hh, j, pl.ds(seg, sl, stride=nseg), :]
                              for seg in range(seg0, seg0 + mmg) for hh in range(nsets)], axis=0)
             for j in range(nslab)], axis=1)
        y = _dot(h.astype(bf16), wout_ref[...])
        y = y + y1_ref[pl.ds(seg0 * seg_len, mmg * seg_len), :]
        for g, seg in enumerate(range(seg0, seg0 + mmg)):
            tok0 = seg * seg_tok
            for r in range(lc):
                y_ref[pl.ds(tok0 + r, seg_len, stride=lc), :] = \
                    y[g * seg_len:(g + 1) * seg_len, r * LANES:(r + 1) * LANES]

    def phase_d(seg0):
        rows = pl.ds(seg0 * seg_tok, mmg * seg_tok)
        z_ref[rows, :] = _gelu_tanh(y_ref[rows, :] + d * u_ref[rows, :]).astype(z_ref.dtype)

    groups = list(range(0, nseg, mmg))
    phase_c(groups[0])
    for gi, seg0 in enumerate(groups):
        if gi + 1 < len(groups):
            phase_c(groups[gi + 1])
        phase_d(seg0)


def _s5_core(u, wcs, wout, wtz, adec, d_skip, lc):
    B, nb, S, _ = u.shape
    nseg = S5_SEGMENTS
    m = S // lc
    seg_len = m // nseg
    assert seg_len * nseg * lc == S and seg_len & (seg_len - 1) == 0
    nslab = wcs.shape[-1] // LANES
    kw = wcs.shape[1]
    kern = functools.partial(_s5_core_kernel, lc=lc, nseg=nseg, seg_len=seg_len)
    return pl.pallas_call(
        kern,
        grid=(nb, B),
        in_specs=[
            pl.BlockSpec((None, None, S, LANES), lambda k, b: (b, k, 0, 0)),
            pl.BlockSpec((None, kw, nslab * LANES), lambda k, b: (k, 0, 0)),
            pl.BlockSpec((None, nslab * LANES, kw), lambda k, b: (k, 0, 0)),
            pl.BlockSpec((None, kw, kw), lambda k, b: (k, 0, 0)),
            pl.BlockSpec((None, 2, adec.shape[-1]), lambda k, b: (k, 0, 0)),
            pl.BlockSpec((1, LANES), lambda k, b: (0, k)),
        ],
        out_specs=pl.BlockSpec((None, None, S, LANES), lambda k, b: (b, k, 0, 0)),
        out_shape=jax.ShapeDtypeStruct((B, nb, S, LANES), bf16),
        scratch_shapes=[
            pltpu.VMEM((S5_SCAN_SETS, nslab, nseg * seg_len // S5_SCAN_SETS, LANES), f32),
            pltpu.VMEM((m, kw), f32),
            pltpu.VMEM((S, LANES), f32),
        ],
        compiler_params=_cparams("parallel", "parallel"),
        name="s5_core",
    )(u, wcs, wout, wtz, adec, d_skip)


def _s5_out_kernel(z_ref, x_ref, mod_ref, wg_ref, wo_ref, h_ref):
    subs = _subtiles(x_ref.shape[0])
    gate = mod_ref[2:3, :]

    def z_rows(rows):
        return jnp.concatenate([z_ref[k, rows, :] for k in range(z_ref.shape[0])], axis=1)

    def glu_logits(rows):
        return _dot(z_rows(rows), wg_ref[...])

    g_next = glu_logits(subs[0])
    for k, rows in enumerate(subs):
        g = g_next
        if k + 1 < len(subs):
            g_next = glu_logits(subs[k + 1])
        z2 = z_rows(rows).astype(f32) * (1.0 / (1.0 + jnp.exp(-g)))
        m = _dot(z2.astype(bf16), wo_ref[...])
        h_ref[rows, :] = x_ref[rows, :] + gate * m


def _s5_out(z, x, mod4, l, wg_bf, wo_bf, tm):
    B, S, D = x.shape
    row = pl.BlockSpec((None, tm, D), lambda b, i: (b, i, 0))
    zblk = pl.BlockSpec((None, D // LANES, tm, LANES), lambda b, i: (b, 0, i, 0))
    return pl.pallas_call(
        _s5_out_kernel,
        grid=(B, S // tm),
        in_specs=[zblk, row, _mod_spec(D)(l), _resident((D, D)), _resident((D, D))],
        out_specs=row,
        out_shape=jax.ShapeDtypeStruct((B, S, D), f32),
        compiler_params=_cparams("parallel", "parallel"),
        name="s5_out",
    )(z, x, mod4, wg_bf, wo_bf)


def _ffn_mix_kernel(mo_ref, mohalo_ref, mmod_ref, wm_ref, h_ref, halo_ref, *rest, fc, final_norm):
    x_ref = rest[-1]
    gate_m = mmod_ref[2:3, :]
    hb = halo_ref.shape[0]
    x_ref[0:hb, :] = halo_ref[...] + gate_m * _dot(mohalo_ref[...], wm_ref[...])
    for rows in _subtiles(h_ref.shape[0]):
        x_ref[hb + rows.start:hb + rows.stop, :] = (
            h_ref[rows, :] + gate_m * _dot(mo_ref[rows, :], wm_ref[...]))
    tm = h_ref.shape[0]
    _ffn_kernel(x_ref.at[pl.ds(hb, tm)], x_ref.at[pl.ds(0, hb)], *rest[:-1],
                fc=fc, final_norm=final_norm)


def _ffn_kernel(h_ref, halo_ref, mod_ref, g_ref, wu_ref, wd_ref, cw_ref, cb_ref,
                fg_ref, o_ref, hn_ref, a_ref, b_ref, gt_ref, *, fc, final_norm):
    i = pl.program_id(1)
    tm = h_ref.shape[0]
    hb = halo_ref.shape[0]
    taps, F = cw_ref.shape
    nc = F // fc
    g, shift, scale = g_ref[...], mod_ref[0:1, :], mod_ref[1:2, :]
    x = h_ref[...]
    keep = (i > 0).astype(f32)
    hn_ref[0:hb, :] = (_modulate(halo_ref[...], g, shift, scale) * keep).astype(bf16)
    a_ref[0, 0:hb, :] = _dot(hn_ref[0:hb, :], wu_ref[:, 0:fc])
    for rows in _subtiles(tm):
        hrows = slice(hb + rows.start, hb + rows.stop)
        hn = _modulate(h_ref[rows, :], g, shift, scale).astype(bf16)
        hn_ref[hrows, :] = hn
        a_ref[0, hrows, :] = _dot(hn, wu_ref[:, 0:fc])
        b_ref[0, rows, :] = _dot(hn, wu_ref[:, F:F + fc])

    def up(j, slot):
        cols = slice(j * fc, (j + 1) * fc)
        a_ref[slot] = _dot(hn_ref[...], wu_ref[:, cols])
        b_ref[slot] = _dot(hn_ref[hb:, :], wu_ref[:, F + j * fc:F + (j + 1) * fc])

    def gate(j, slot):
        cols = slice(j * fc, (j + 1) * fc)
        cw = cw_ref[:, cols]
        ac = cb_ref[:, cols]
        for t in range(taps):
            off = hb - (taps - 1) + t
            ac = ac + cw[t:t + 1, :] * a_ref[slot, off:off + tm, :]
        gt_ref[:, cols] = (_silu(ac) * b_ref[slot]).astype(bf16)

    for j in range(nc):
        if j + 1 < nc:
            up(j + 1, (j + 1) % 2)
        gate(j, j % 2)
    out = x + mod_ref[2:3, :] * _dot(gt_ref[...], wd_ref[...])
    if final_norm:
        ms = jnp.mean(out * out, axis=-1, keepdims=True)
        out = out * lax.rsqrt(ms + EPS) * fg_ref[...]
    o_ref[...] = out


def _ffn(h, mod4, l, g, w_up, conv_w, conv_b, w_down, final_g, tm, fc, final_norm):
    B, S, D = h.shape
    F = w_down.shape[0]
    taps = conv_w.shape[0]
    assert F % fc == 0 and F % LANES == 0
    hb = BF16_ROWS
    per = tm // hb
    kern = functools.partial(_ffn_kernel, fc=fc, final_norm=final_norm)
    return pl.pallas_call(
        kern,
        grid=(B, S // tm),
        in_specs=[
            pl.BlockSpec((None, tm, D), lambda b, i: (b, i, 0)),
            pl.BlockSpec((None, hb, D), lambda b, i: (b, jnp.maximum(i * per - 1, 0), 0)),
            _mod_spec(D)(l),
            _resident((1, D)),
            _resident((D, 2 * F)),
            _resident((F, D)),
            _resident((taps, F)),
            _resident((1, F)),
            _resident((1, D)),
        ],
        out_specs=pl.BlockSpec((None, tm, D), lambda b, i: (b, i, 0)),
        out_shape=jax.ShapeDtypeStruct((B, S, D), f32),
        scratch_shapes=[
            pltpu.VMEM((hb + tm, D), bf16),
            pltpu.VMEM((2, hb + tm, fc), f32),
            pltpu.VMEM((2, tm, fc), f32),
            pltpu.VMEM((tm, F), bf16),
        ],
        compiler_params=_cparams("parallel", "parallel"),
        name="ffn",
    )(h, h, mod4, g, w_up.astype(bf16), w_down.astype(bf16), conv_w, conv_b.reshape(1, F),
      final_g)


def _ffn_after_attn(o, h, mod4, l_mix, w_mix_bf, l, g, w_up, conv_w, conv_b, w_down, final_g,
                    fc, final_norm):
    B, S, D = h.shape
    F = w_down.shape[0]
    taps = conv_w.shape[0]
    tm = o.shape[3]
    nq = S // tm
    hb = BF16_ROWS
    per = tm // hb
    assert F % fc == 0 and F % LANES == 0

    def pair_of(q):
        return jnp.where(q < nq // 2, q, nq - 1 - q), (q >= nq // 2).astype(jnp.int32)

    def o_tile(b, i):
        pr, sl = pair_of(i)
        return (b, pr, sl, 0, 0)

    def o_halo(b, i):
        pr, sl = pair_of(jnp.maximum(i - 1, 0))
        return (b, pr, sl, per - 1, 0)

    kern = functools.partial(_ffn_mix_kernel, fc=fc, final_norm=final_norm)
    return pl.pallas_call(
        kern,
        grid=(B, nq),
        in_specs=[
            pl.BlockSpec((None, None, None, tm, D), o_tile),
            pl.BlockSpec((None, None, None, hb, D), o_halo),
            _mod_spec(D)(l_mix),
            _resident((D, D)),
            pl.BlockSpec((None, tm, D), lambda b, i: (b, i, 0)),
            pl.BlockSpec((None, hb, D), lambda b, i: (b, jnp.maximum(i * per - 1, 0), 0)),
            _mod_spec(D)(l),
            _resident((1, D)),
            _resident((D, 2 * F)),
            _resident((F, D)),
            _resident((taps, F)),
            _resident((1, F)),
            _resident((1, D)),
        ],
        out_specs=pl.BlockSpec((None, tm, D), lambda b, i: (b, i, 0)),
        out_shape=jax.ShapeDtypeStruct((B, S, D), f32),
        scratch_shapes=[
            pltpu.VMEM((hb + tm, D), bf16),
            pltpu.VMEM((2, hb + tm, fc), f32),
            pltpu.VMEM((2, tm, fc), f32),
            pltpu.VMEM((tm, F), bf16),
            pltpu.VMEM((hb + tm, D), f32),
        ],
        compiler_params=_cparams("parallel", "parallel"),
        name="ffn_mix",
    )(o, o, mod4, w_mix_bf, h, h, mod4, g, w_up.astype(bf16), w_down.astype(bf16), conv_w,
      conv_b.reshape(1, F), final_g)


LOG2E = 1.4426950408889634


FBIAS_PIECES = 3


def _fbias_selector(n_heads):
    heads = 2
    sel = np.zeros((FBIAS_PIECES * LANES, (n_heads // heads) * LANES), np.float32)
    for h in range(n_heads):
        for p in range(FBIAS_PIECES):
            sel[p * LANES + h, (h // heads) * LANES + (h % heads) * FBIAS_PIECES + p] = 1.0
    return jnp.asarray(sel, dtype=bf16)


def _fox_in_kernel(h_ref, mod_ref, g_ref, wqt_ref, wk_ref, wvt_ref, wf_ref, bf_ref, sel_ref,
                   qt_ref, k_ref, vt_ref, fa_ref, carry_ref, *, q_scale, ts):
    i = pl.program_id(1)
    tm, D = h_ref.shape

    @pl.when(i == 0)
    def _():
        carry_ref[...] = jnp.zeros_like(carry_ref)

    g, shift, scale = g_ref[...], mod_ref[0:1, :], mod_ref[1:2, :]
    subs = _subtiles(tm)
    sub = subs[0].stop
    r = lax.broadcasted_iota(jnp.int32, (sub, sub), 0)
    c = lax.broadcasted_iota(jnp.int32, (sub, sub), 1)
    tri = (c <= r).astype(bf16)
    carry = carry_ref[...]
    for rows in subs:
        hn = _modulate(h_ref[rows, :], g, shift, scale).astype(bf16)
        k_ref[rows, :] = _dot(hn, wk_ref[...]).astype(bf16)
        qt_ref[:, rows] = (_dot_nt(wqt_ref[...], hn) * q_scale).astype(bf16)
        vt = _dot_nt(wvt_ref[...], hn).astype(bf16)
        for cc in range(sub // ts):
            vt_ref[rows.start // ts + cc] = vt[:, cc * ts:(cc + 1) * ts]

        fl = _dot(hn, wf_ref[...]) + bf_ref[...]
        lf = jnp.minimum(fl, 0.0) - jnp.log(1.0 + jnp.exp(-jnp.abs(fl)))
        p0, p1, p2 = _split3(lf)
        cs = _dot(tri, p0) + _dot(tri, p1) + _dot(tri, p2) + carry
        carry = cs[sub - 1:sub, :]
        fa = _dot(jnp.concatenate(_split3(cs * LOG2E), axis=1), sel_ref[...]).astype(bf16)
        for hp in range(fa_ref.shape[0]):
            fa_ref[hp, rows, :] = fa[:, hp * LANES:(hp + 1) * LANES]
    carry_ref[...] = carry


def _fox_in(h, mod4, l, g, wqt_bf, wk_bf, wvt_bf, wf_bf, bf_pad, sel, q_scale, tm, ts):
    B, S, D = h.shape
    nhp = sel.shape[1] // LANES
    return pl.pallas_call(
        functools.partial(_fox_in_kernel, q_scale=q_scale, ts=ts),
        grid=(B, S // tm),
        in_specs=[
            pl.BlockSpec((None, tm, D), lambda b, i: (b, i, 0)),
            _mod_spec(D)(l),
            _resident((1, D)),
            _resident((D, D)),
            _resident((D, D)),
            _resident((D, D)),
            _resident((D, LANES)),
            _resident((1, LANES)),
            _resident(sel.shape),
        ],
        out_specs=[
            pl.BlockSpec((None, D, tm), lambda b, i: (b, 0, i)),
            pl.BlockSpec((None, tm, D), lambda b, i: (b, i, 0)),
            pl.BlockSpec((None, tm // ts, D, ts), lambda b, i: (b, i, 0, 0)),
            pl.BlockSpec((None, nhp, tm, LANES), lambda b, i: (b, 0, i, 0)),
        ],
        out_shape=[
            jax.ShapeDtypeStruct((B, D, S), bf16),
            jax.ShapeDtypeStruct((B, S, D), bf16),
            jax.ShapeDtypeStruct((B, S // ts, D, ts), bf16),
            jax.ShapeDtypeStruct((B, nhp, S, LANES), bf16),
        ],
        scratch_shapes=[pltpu.VMEM((1, LANES), f32)],
        compiler_params=_cparams("parallel", "arbitrary"),
        name="fox_in",
    )(h, mod4, g, wqt_bf, wk_bf, wvt_bf, wf_bf, bf_pad, sel)


NEG_BIG = -1e30
ATTN_PAIR = 2
ATTN_GROUPS_PER_TRIP = 4


def _sublane_allreduce(x, op):
    for sh in (4, 2, 1):
        x = op(x, pltpu.roll(x, sh, axis=0))
    return x


def _tree(xs, op):
    while len(xs) > 1:
        xs = [op(xs[i], xs[i + 1]) if i + 1 < len(xs) else xs[i] for i in range(0, len(xs), 2)]
    return xs[0]


def _attn_kernel(qta_ref, qtb_ref, k_ref, fa_ref, vt_ref, o_ref, s_ref, mx_ref, *, ts, dh):
    i = pl.program_id(2)
    nq = 2 * pl.num_programs(2)
    _attn_block(i, qta_ref, k_ref, fa_ref, vt_ref, o_ref.at[0], s_ref, mx_ref, ts=ts, dh=dh)
    _attn_block(nq - 1 - i, qtb_ref, k_ref, fa_ref, vt_ref, o_ref.at[1], s_ref, mx_ref,
                ts=ts, dh=dh)


def _attn_block(qi, qt_ref, k_ref, fa_ref, vt_ref, o_ref, s_ref, mx_ref, *, ts, dh):
    tq = qt_ref.shape[1]
    heads = LANES // dh
    spq = tq // ts

    qt = qt_ref[...]
    rowi = lax.broadcasted_iota(jnp.int32, qt.shape, 0)
    qts = []
    for e in range(heads):
        own_f = (rowi // FBIAS_PIECES) == e
        qts.append(jnp.concatenate(
            [jnp.where((rowi // dh) == e, qt, jnp.zeros_like(qt)),
             jnp.where(own_f, -1.0, 0.0).astype(bf16)], axis=0))
    nrt = ts // SUBLANES

    def issue(g, bank):
        gk = spq * ts
        ks = pl.ds(pl.multiple_of(g * gk, gk), gk)
        kb = jnp.concatenate([k_ref[ks, :], fa_ref[ks, :]], axis=1)
        for e in range(heads):
            s = _dot(kb, qts[e])
            for d in range(spq):
                sd = s[d * ts:(d + 1) * ts, :]
                s_ref[bank, d, e] = sd
                mx_ref[bank, d, e] = _tree([sd[r * SUBLANES:(r + 1) * SUBLANES, :]
                                            for r in range(nrt)], jnp.maximum)

    ones_rows = (lax.broadcasted_iota(jnp.int32, (BF16_ROWS, ATTN_PAIR * ts), 0) == 0).astype(bf16)
    acc_rows = dh + SUBLANES

    def absorb(g, bank, ds, carry, masked):
        vtb = jnp.concatenate([vt_ref[g * spq + d] for d in ds], axis=1)
        lane0 = ds[0] * ts if masked else 0
        out = []
        for e in range(heads):
            m_all, acc_all = carry[2 * e:2 * e + 2]
            m, acc = m_all[:, lane0:], acc_all[:, lane0:]
            if not masked:
                tiles = None
                mx = _tree([mx_ref[bank, d, e] for d in ds], jnp.maximum)
            else:
                tiles = []
                for d in ds:
                    kidx = lax.broadcasted_iota(jnp.int32, (ts, tq - lane0), 0) + d * ts
                    qidx = lax.broadcasted_iota(jnp.int32, (ts, tq - lane0), 1) + lane0
                    sm = jnp.where(kidx <= qidx, s_ref[bank, d, e, :, lane0:], NEG_BIG)
                    tiles += [sm[r * SUBLANES:(r + 1) * SUBLANES, :] for r in range(nrt)]
                mx = _tree(tiles, jnp.maximum)
            m_new = jnp.maximum(m, _sublane_allreduce(mx, jnp.maximum))
            alpha = jnp.exp2(m - m_new)
            if tiles is None:
                tiles = [s_ref[bank, d, e, r * SUBLANES:(r + 1) * SUBLANES, :]
                         for d in ds for r in range(nrt)]
            p = jnp.concatenate([jnp.exp2(t - m_new) for t in tiles], axis=0).astype(bf16)
            lhs = jnp.concatenate([vtb[e * dh:(e + 1) * dh, :], ones_rows], axis=0)
            pv = _dot(lhs, p)[:acc_rows, :]
            acc = jnp.concatenate([alpha] * (acc_rows // SUBLANES), axis=0) * acc + pv
            if lane0:
                m_new = jnp.concatenate([m_all[:, :lane0], m_new], axis=1)
                acc = jnp.concatenate([acc_all[:, :lane0], acc], axis=1)
            out += [m_new, acc]
        return tuple(out)

    assert spq % ATTN_PAIR == 0

    def group(g, bank, carry, masked):
        for d0 in range(0, spq, ATTN_PAIR):
            carry = absorb(g, bank, list(range(d0, d0 + ATTN_PAIR)), carry, masked)
        return carry

    init = []
    for e in range(heads):
        init += [jnp.full((SUBLANES, tq), NEG_BIG, f32), jnp.zeros((acc_rows, tq), f32)]
    issue(0, 0)

    gpt = ATTN_GROUPS_PER_TRIP

    def trip(t, c):
        g0 = gpt * t
        for k in range(gpt):
            issue(g0 + k + 1, (k + 1) % 2)
            c = group(g0 + k, k % 2, c, False)
        return c

    carry = lax.fori_loop(0, qi // gpt, trip, tuple(init))
    g0 = qi - qi % gpt

    def tail(r):
        def run(c):
            for k in range(r):
                issue(g0 + k + 1, (k + 1) % 2)
                c = group(g0 + k, k % 2, c, False)
            return group(g0 + r, r % 2, c, True)
        return run

    carry = lax.switch(qi % gpt, [tail(r) for r in range(gpt)], carry)
    outs = []
    for e in range(heads):
        acc = carry[2 * e + 1]
        l = jnp.broadcast_to(acc[dh:dh + 1, :], (dh, tq))
        outs.append(acc[:dh, :] / l)
    o_ref[...] = jnp.concatenate(outs, axis=0).T.astype(o_ref.dtype)


def _attn(qt, k, fa, vt, n_heads, tq):
    B, D, S = qt.shape
    ts = vt.shape[-1]
    dh = D // n_heads
    heads = LANES // dh
    nhp = D // LANES
    assert heads * FBIAS_PIECES <= LANES and heads == 2
    nq = S // tq
    assert nq % 2 == 0
    kern = functools.partial(_attn_kernel, ts=ts, dh=dh)
    return pl.pallas_call(
        kern,
        grid=(B, nhp, nq // 2),
        in_specs=[
            pl.BlockSpec((None, LANES, tq), lambda b, h, i: (b, h, i)),
            pl.BlockSpec((None, LANES, tq), lambda b, h, i: (b, h, nq - 1 - i)),
            pl.BlockSpec((None, S, LANES), lambda b, h, i: (b, 0, h)),
            pl.BlockSpec((None, None, S, LANES), lambda b, h, i: (b, h, 0, 0)),
            pl.BlockSpec((None, S // ts, LANES, ts), lambda b, h, i: (b, 0, h, 0)),
        ],
        out_specs=pl.BlockSpec((None, None, 2, tq, LANES), lambda b, h, i: (b, i, 0, 0, h)),
        out_shape=jax.ShapeDtypeStruct((B, nq // 2, 2, tq, D), bf16),
        scratch_shapes=[pltpu.VMEM((2, tq // ts, heads, ts, tq), f32),
                        pltpu.VMEM((2, tq // ts, heads, SUBLANES, tq), f32)],
        compiler_params=_cparams("parallel", "parallel", "arbitrary"),
        name="attn",
    )(qt, qt, k, fa, vt)


def kernel(x, c, norm_g, ada_w, ada_b, s5_w_in, s5_lam_re, s5_lam_im, s5_log_dt, s5_b_re, s5_b_im, s5_c_re, s5_c_im, s5_d, s5_w_glu, s5_w_out, fox_w_in, fox_b_f, fox_w_out, ffn_w_up, ffn_conv_w, ffn_conv_b, ffn_w_down, final_g):
    B, S, D = x.shape
    depth = norm_g.shape[0]
    n_heads = fox_b_f.shape[-1]
    tm = min(1024, S)
    tmm = min(1024, S)
    fc = 256
    final = final_g.reshape(1, D)

    mod = _adaln(c, ada_w, ada_b)
    mod4 = mod.reshape(depth * 2, B, 3, D)

    h = x
    for i in range(depth):
        j = i // 2
        g_mix = norm_g[i, 0].reshape(1, D)
        g_ffn = norm_g[i, 1].reshape(1, D)
        if i % 2 == 0:
            u = _s5_in(h, mod4, 2 * i, g_mix, s5_w_in[j].astype(bf16), tmm)
            wcs, wout, wtz, adec = _s5_weights(s5_lam_re[j], s5_lam_im[j], s5_log_dt[j],
                                               s5_b_re[j], s5_b_im[j], s5_c_re[j], s5_c_im[j],
                                               S5_CHUNK)
            z = _s5_core(u, wcs, wout, wtz, adec, s5_d[j].reshape(1, D), S5_CHUNK)
            h = _s5_out(z, h, mod4, 2 * i, s5_w_glu[j].astype(bf16),
                        s5_w_out[j].astype(bf16), tmm)
        else:
            w_in = fox_w_in[j]
            wqt = w_in[:, :D].T.astype(bf16)
            wk = w_in[:, D:2 * D].astype(bf16)
            wvt = w_in[:, 2 * D:3 * D].T.astype(bf16)
            wf = jnp.pad(w_in[:, 3 * D:], ((0, 0), (0, LANES - n_heads))).astype(bf16)
            bfp = jnp.pad(fox_b_f[j], (0, LANES - n_heads)).reshape(1, LANES)
            ts = LANES
            q_scale = float((D // n_heads) ** -0.5 * LOG2E)
            qt, k, vt, fa = _fox_in(h, mod4, 2 * i, g_mix, wqt, wk, wvt, wf, bfp,
                                    _fbias_selector(n_heads), q_scale, tmm, ts)
            o = _attn(qt, k, fa, vt, n_heads, tq=min(512, S))
            h = _ffn_after_attn(o, h, mod4, 2 * i, fox_w_out[j].astype(bf16), 2 * i + 1, g_ffn,
                                ffn_w_up[i], ffn_conv_w[i], ffn_conv_b[i], ffn_w_down[i], final,
                                fc, final_norm=(i == depth - 1))
            continue
        h = _ffn(h, mod4, 2 * i + 1, g_ffn, ffn_w_up[i], ffn_conv_w[i], ffn_conv_b[i],
                 ffn_w_down[i], final, tm, fc, final_norm=(i == depth - 1))
    return h
```

```python
import functools

import jax
import jax.numpy as jnp
import numpy as np
from jax import lax
from jax.experimental import pallas as pl
from jax.experimental.pallas import tpu as pltpu

EPS = 1e-6
LANES = 128
SUBLANES = 8
BF16_ROWS = 16
VMEM_LIMIT = 56 * 1024 * 1024

S5_CHUNK = 4
S5_SEGMENTS = SUBLANES
S5_SCAN_UNROLL = 4
S5_SCAN_SETS = 2
S5_MM_SEGMENTS = 4

f32 = jnp.float32
bf16 = jnp.bfloat16


def _cparams(*sem):
    return pltpu.CompilerParams(dimension_semantics=sem, vmem_limit_bytes=VMEM_LIMIT)


def _resident(shape):
    nd = len(shape)
    return pl.BlockSpec(shape, lambda *_: (0,) * nd, pipeline_mode=pl.Buffered(1))


def _dot(a, b):
    return jnp.dot(a, b, preferred_element_type=f32)


def _dot_nt(a, b):
    return lax.dot_general(a, b, (((1,), (1,)), ((), ())), preferred_element_type=f32)


SUB_ROWS = 256


def _subtiles(tm):
    sub = min(SUB_ROWS, tm)
    assert tm % sub == 0
    return [slice(k * sub, (k + 1) * sub) for k in range(tm // sub)]


def _split3(x):
    hi = x.astype(bf16)
    r1 = x - hi.astype(f32)
    mid = r1.astype(bf16)
    lo = (r1 - mid.astype(f32)).astype(bf16)
    return hi, mid, lo


def _modulate(x, g, shift, scale):
    ms = jnp.mean(x * x, axis=-1, keepdims=True)
    return x * lax.rsqrt(ms + EPS) * g * (1.0 + scale) + shift


def _silu(x):
    return x * (1.0 / (1.0 + jnp.exp(-x)))


def _gelu_tanh(x):
    c = 0.7978845608028654
    return 0.5 * x * (1.0 + jnp.tanh(c * (x + 0.044715 * (x * x * x))))


def _adaln_kernel(ct_ref, w_ref, b_ref, o_ref):
    ct = ct_ref[...]
    cs = _silu(ct)
    D = ct.shape[0]
    rc = min(SUB_ROWS, D)
    for b in range(ct.shape[1]):
        acc = b_ref[...]
        for r0 in range(0, D, rc):
            acc = acc + jnp.sum(w_ref[r0:r0 + rc, :] * cs[r0:r0 + rc, b:b + 1], axis=0,
                                keepdims=True)
        o_ref[b:b + 1, :] = acc


def _adaln(c, ada_w, ada_b):
    B, D = c.shape
    L = ada_w.shape[0] * ada_w.shape[1]
    N = ada_w.shape[-1]
    tn = 1024 if N % 1024 == 0 else 512
    w = ada_w.reshape(L, D, N)
    bias = ada_b.reshape(L, 1, N)
    return pl.pallas_call(
        _adaln_kernel,
        grid=(L, N // tn),
        in_specs=[
            pl.BlockSpec((D, B), lambda l, j: (0, 0)),
            pl.BlockSpec((None, D, tn), lambda l, j: (l, 0, j)),
            pl.BlockSpec((None, 1, tn), lambda l, j: (l, 0, j)),
        ],
        out_specs=pl.BlockSpec((None, B, tn), lambda l, j: (l, 0, j)),
        out_shape=jax.ShapeDtypeStruct((L, B, N), f32),
        compiler_params=_cparams("parallel", "parallel"),
        name="adaln",
    )(c.T, w, bias)


def _mod_spec(D):
    return lambda l: pl.BlockSpec((None, None, 3, D), lambda b, i: (l, b, 0, 0))


def _s5_in_kernel(x_ref, mod_ref, g_ref, w_ref, u_ref):
    g, shift, scale = g_ref[...], mod_ref[0:1, :], mod_ref[1:2, :]
    for rows in _subtiles(x_ref.shape[0]):
        hn = _modulate(x_ref[rows, :], g, shift, scale)
        u = _dot(hn.astype(bf16), w_ref[...])
        for k in range(u_ref.shape[0]):
            u_ref[k, rows, :] = u[:, k * LANES:(k + 1) * LANES]


def _s5_in(x, mod4, l, g, w_bf, tm):
    B, S, D = x.shape
    return pl.pallas_call(
        _s5_in_kernel,
        grid=(B, S // tm),
        in_specs=[
            pl.BlockSpec((None, tm, D), lambda b, i: (b, i, 0)),
            _mod_spec(D)(l),
            _resident((1, D)),
            _resident((D, D)),
        ],
        out_specs=pl.BlockSpec((None, D // LANES, tm, LANES), lambda b, i: (b, 0, i, 0)),
        out_shape=jax.ShapeDtypeStruct((B, D // LANES, S, LANES), f32),
        compiler_params=_cparams("parallel", "parallel"),
        name="s5_in",
    )(x, mod4, g, w_bf)


def _s5_weights_kernel(lre_ref, lim_ref, ldt_ref, br_ref, bi_ref, cr_ref, ci_ref,
                       wcs_ref, wout_ref, wtz_ref, adec_ref, *, lc, cg, p_state):
    lre = lre_ref[...]
    lim = lim_ref[...]
    dt = jnp.exp(ldt_ref[...])
    nl = lre.shape[1]
    mag = jnp.exp(lre * dt)
    lam_r = mag * jnp.cos(lim * dt)
    lam_i = mag * jnp.sin(lim * dt)
    num_r = lam_r - 1.0
    den = lre * lre + lim * lim
    k_r = (num_r * lre + lam_i * lim) / den
    k_i = (lam_i * lre - num_r * lim) / den
    br = br_ref[...]
    bi = bi_ref[...]
    bb_r = k_r * br - k_i * bi
    bb_i = k_r * bi + k_i * br
    cr = cr_ref[...]
    ci = ci_ref[...]

    ngrp = nl // p_state
    rows = ngrp * cg
    row_grp = lax.broadcasted_iota(jnp.int32, (rows, nl), 0) // cg
    lane_grp = lax.broadcasted_iota(jnp.int32, (rows, nl), 1) // p_state
    own = row_grp == lane_grp

    def slab(xr, xi):
        tr = jnp.where(own, jnp.tile(xr, (ngrp, 1)), 0.0)
        ti = jnp.where(own, jnp.tile(xi, (ngrp, 1)), 0.0)
        return jnp.concatenate([tr, ti], axis=1)

    pw = [(jnp.ones_like(lam_r), jnp.zeros_like(lam_r))]
    for _ in range(lc):
        pr, pi = pw[-1]
        pw.append((pr * lam_r - pi * lam_i, pr * lam_i + pi * lam_r))
    adec_ref[0:1, :] = pw[lc][0]
    adec_ref[1:2, :] = pw[lc][1]

    for s in range(lc):
        pr, pi = pw[lc - 1 - s]
        wcs_ref[s * rows:(s + 1) * rows, :] = slab(pr * bb_r - pi * bb_i,
                                                   pr * bb_i + pi * bb_r).astype(bf16)

    wq = []
    for t in range(lc + 1):
        pr, pi = pw[t]
        wq.append(slab(cr * pr - ci * pi, -(cr * pi + ci * pr)))
    for r in range(lc):
        wout_ref[:, r * rows:(r + 1) * rows] = wq[r + 1].T.astype(bf16)

    bsl = _split3(slab(bb_r, bb_i))
    kt = []
    for t in range(lc):
        q = _split3(wq[t])
        acc = _dot_nt(bsl[0], q[0])
        acc += _dot_nt(bsl[0], q[1]) + _dot_nt(bsl[1], q[0])
        acc += _dot_nt(bsl[1], q[1]) + _dot_nt(bsl[0], q[2]) + _dot_nt(bsl[2], q[0])
        kt.append(acc)
    zero = jnp.zeros((rows, rows), f32)
    for s in range(lc):
        for r in range(lc):
            blk = kt[r - s] if r >= s else zero
            wtz_ref[s * rows:(s + 1) * rows, r * rows:(r + 1) * rows] = blk.astype(bf16)


def _s5_weights(lam_re, lam_im, log_dt, b_re, b_im, c_re, c_im, lc):
    G, P = lam_re.shape
    Cg = b_re.shape[-1]
    gb = LANES // Cg
    nb = G // gb
    nl = gb * P
    rows = gb * Cg
    lre = lam_re.reshape(nb, 1, nl)
    lim = lam_im.reshape(nb, 1, nl)
    ldt = jnp.repeat(log_dt, P).reshape(nb, 1, nl)
    brt = b_re.reshape(nb, gb, P, Cg).transpose(0, 3, 1, 2).reshape(nb, Cg, nl)
    bit = b_im.reshape(nb, gb, P, Cg).transpose(0, 3, 1, 2).reshape(nb, Cg, nl)
    crt = c_re.reshape(nb, gb, Cg, P).transpose(0, 2, 1, 3).reshape(nb, Cg, nl)
    cit = c_im.reshape(nb, gb, Cg, P).transpose(0, 2, 1, 3).reshape(nb, Cg, nl)
    vec = pl.BlockSpec((None, 1, nl), lambda k: (k, 0, 0))
    mat = pl.BlockSpec((None, Cg, nl), lambda k: (k, 0, 0))
    return pl.pallas_call(
        functools.partial(_s5_weights_kernel, lc=lc, cg=Cg, p_state=P),
        grid=(nb,),
        in_specs=[vec, vec, vec, mat, mat, mat, mat],
        out_specs=[
            pl.BlockSpec((None, lc * rows, 2 * nl), lambda k: (k, 0, 0)),
            pl.BlockSpec((None, 2 * nl, lc * rows), lambda k: (k, 0, 0)),
            pl.BlockSpec((None, lc * rows, lc * rows), lambda k: (k, 0, 0)),
            pl.BlockSpec((None, 2, nl), lambda k: (k, 0, 0)),
        ],
        out_shape=[
            jax.ShapeDtypeStruct((nb, lc * rows, 2 * nl), bf16),
            jax.ShapeDtypeStruct((nb, 2 * nl, lc * rows), bf16),
            jax.ShapeDtypeStruct((nb, lc * rows, lc * rows), bf16),
            jax.ShapeDtypeStruct((nb, 2, nl), f32),
        ],
        compiler_params=_cparams("parallel"),
        name="s5_weights",
    )(lre, lim, ldt, brt, bit, crt, cit)


def _s5_core_kernel(u_ref, wcs_ref, wout_ref, wtz_ref, adec_ref, d_ref, z_ref,
                    st_ref, y1_ref, y_ref, *, lc, nseg, seg_len):
    nsets, nslab = st_ref.shape[0], st_ref.shape[1]
    half = nslab // 2
    seg_tok = seg_len * lc
    sl = seg_len // nsets

    mmg = min(S5_MM_SEGMENTS, nseg)

    def phase_a(seg0):
        xs = []
        for seg in range(seg0, seg0 + mmg):
            tok0 = seg * seg_tok
            xs.append(jnp.concatenate(
                [u_ref[pl.ds(tok0 + s, seg_len, stride=lc), :] for s in range(lc)], axis=1))
        xb = jnp.concatenate(xs, axis=0).astype(bf16)
        e = _dot(xb, wcs_ref[...])
        for g, seg in enumerate(range(seg0, seg0 + mmg)):
            for h in range(nsets):
                r0 = g * seg_len + h * sl
                for j in range(nslab):
                    st_ref[h, j, pl.ds(seg, sl, stride=nseg), :] = \
                        e[r0:r0 + sl, j * LANES:(j + 1) * LANES]
        y1_ref[pl.ds(seg0 * seg_len, mmg * seg_len), :] = _dot(xb, wtz_ref[...])

    for seg0 in range(0, nseg, mmg):
        phase_a(seg0)

    a_r = [jnp.broadcast_to(adec_ref[0:1, j * LANES:(j + 1) * LANES], (nseg, LANES))
           for j in range(half)]
    a_i = [jnp.broadcast_to(adec_ref[1:2, j * LANES:(j + 1) * LANES], (nseg, LANES))
           for j in range(half)]

    def step(i, hs, store):
        out = []
        rows = pl.ds(pl.multiple_of(i * nseg, nseg), nseg)
        for h in range(nsets):
            for j in range(half):
                hr, hi = hs[h * nslab + 2 * j], hs[h * nslab + 2 * j + 1]
                er = st_ref[h, j, rows, :]
                ei = st_ref[h, half + j, rows, :]
                if store:
                    st_ref[h, j, rows, :] = hr
                    st_ref[h, half + j, rows, :] = hi
                out.append(a_r[j] * hr - a_i[j] * hi + er)
                out.append(a_r[j] * hi + a_i[j] * hr + ei)
        return tuple(out)

    zeros = tuple(jnp.zeros((nseg, LANES), f32) for _ in range(nsets * nslab))
    ends = lax.fori_loop(0, sl, lambda i, hs: step(i, hs, False), zeros, unroll=S5_SCAN_UNROLL)

    starts = [None] * (nsets * nslab)
    for j in range(half):
        pr, pi = a_r[j][0:1, :], a_i[j][0:1, :]
        n = sl
        while n > 1:
            pr, pi = pr * pr - pi * pi, 2.0 * pr * pi
            n //= 2
        cr, ci = jnp.zeros((1, LANES), f32), jnp.zeros((1, LANES), f32)
        sr = [[] for _ in range(nsets)]
        si = [[] for _ in range(nsets)]
        for k in range(nseg):
            for h in range(nsets):
                sr[h].append(cr)
                si[h].append(ci)
                er = ends[h * nslab + 2 * j][k:k + 1, :]
                ei = ends[h * nslab + 2 * j + 1][k:k + 1, :]
                cr, ci = pr * cr - pi * ci + er, pr * ci + pi * cr + ei
        for h in range(nsets):
            starts[h * nslab + 2 * j] = jnp.concatenate(sr[h], axis=0)
            starts[h * nslab + 2 * j + 1] = jnp.concatenate(si[h], axis=0)
    lax.fori_loop(0, sl, lambda i, hs: step(i, hs, True), tuple(starts), unroll=S5_SCAN_UNROLL)

    d = d_ref[...]

    def phase_c(seg0):
        h = jnp.concatenate(
            [jnp.concatenate([st_ref[hh, j, pl.ds(seg, sl, stride=nseg), :]
                              for seg in range(seg0, seg0 + mmg) for hh in range(nsets)], axis=0)
             for j in range(nslab)], axis=1)
        y = _dot(h.astype(bf16), wout_ref[...])
        y = y + y1_ref[pl.ds(seg0 * seg_len, mmg * seg_len), :]
        for g, seg in enumerate(range(seg0, seg0 + mmg)):
            tok0 = seg * seg_tok
            for r in range(lc):
                y_ref[pl.ds(tok0 + r, seg_len, stride=lc), :] = \
                    y[g * seg_len:(g + 1) * seg_len, r * LANES:(r + 1) * LANES]

    def phase_d(seg0):
        rows = pl.ds(seg0 * seg_tok, mmg * seg_tok)
        z_ref[rows, :] = _gelu_tanh(y_ref[rows, :] + d * u_ref[rows, :]).astype(z_ref.dtype)

    groups = list(range(0, nseg, mmg))
    phase_c(groups[0])
    for gi, seg0 in enumerate(groups):
        if gi + 1 < len(groups):
            phase_c(groups[gi + 1])
        phase_d(seg0)


def _s5_core(u, wcs, wout, wtz, adec, d_skip, lc):
    B, nb, S, _ = u.shape
    nseg = S5_SEGMENTS
    m = S // lc
    seg_len = m // nseg
    assert seg_len * nseg * lc == S and seg_len & (seg_len - 1) == 0
    nslab = wcs.shape[-1] // LANES
    kw = wcs.shape[1]
    kern = functools.partial(_s5_core_kernel, lc=lc, nseg=nseg, seg_len=seg_len)
    return pl.pallas_call(
        kern,
        grid=(nb, B),
        in_specs=[
            pl.BlockSpec((None, None, S, LANES), lambda k, b: (b, k, 0, 0)),
            pl.BlockSpec((None, kw, nslab * LANES), lambda k, b: (k, 0, 0)),
            pl.BlockSpec((None, nslab * LANES, kw), lambda k, b: (k, 0, 0)),
            pl.BlockSpec((None, kw, kw), lambda k, b: (k, 0, 0)),
            pl.BlockSpec((None, 2, adec.shape[-1]), lambda k, b: (k, 0, 0)),
            pl.BlockSpec((1, LANES), lambda k, b: (0, k)),
        ],
        out_specs=pl.BlockSpec((None, None, S, LANES), lambda k, b: (b, k, 0, 0)),
        out_shape=jax.ShapeDtypeStruct((B, nb, S, LANES), bf16),
        scratch_shapes=[
            pltpu.VMEM((S5_SCAN_SETS, nslab, nseg * seg_len // S5_SCAN_SETS, LANES), f32),
            pltpu.VMEM((m, kw), f32),
            pltpu.VMEM((S, LANES), f32),
        ],
        compiler_params=_cparams("parallel", "parallel"),
        name="s5_core",
    )(u, wcs, wout, wtz, adec, d_skip)


def _s5_out_kernel(z_ref, x_ref, mod_ref, wg_ref, wo_ref, h_ref):
    subs = _subtiles(x_ref.shape[0])
    gate = mod_ref[2:3, :]

    def z_rows(rows):
        return jnp.concatenate([z_ref[k, rows, :] for k in range(z_ref.shape[0])], axis=1)

    def glu_logits(rows):
        return _dot(z_rows(rows), wg_ref[...])

    g_next = glu_logits(subs[0])
    for k, rows in enumerate(subs):
        g = g_next
        if k + 1 < len(subs):
            g_next = glu_logits(subs[k + 1])
        z2 = z_rows(rows).astype(f32) * (1.0 / (1.0 + jnp.exp(-g)))
        m = _dot(z2.astype(bf16), wo_ref[...])
        h_ref[rows, :] = x_ref[rows, :] + gate * m


def _s5_out(z, x, mod4, l, wg_bf, wo_bf, tm):
    B, S, D = x.shape
    row = pl.BlockSpec((None, tm, D), lambda b, i: (b, i, 0))
    zblk = pl.BlockSpec((None, D // LANES, tm, LANES), lambda b, i: (b, 0, i, 0))
    return pl.pallas_call(
        _s5_out_kernel,
        grid=(B, S // tm),
        in_specs=[zblk, row, _mod_spec(D)(l), _resident((D, D)), _resident((D, D))],
        out_specs=row,
        out_shape=jax.ShapeDtypeStruct((B, S, D), f32),
        compiler_params=_cparams("parallel", "parallel"),
        name="s5_out",
    )(z, x, mod4, wg_bf, wo_bf)


def _ffn_mix_kernel(mo_ref, mohalo_ref, mmod_ref, wm_ref, h_ref, halo_ref, *rest, fc, final_norm):
    x_ref = rest[-1]
    gate_m = mmod_ref[2:3, :]
    hb = halo_ref.shape[0]
    x_ref[0:hb, :] = halo_ref[...] + gate_m * _dot(mohalo_ref[...], wm_ref[...])
    for rows in _subtiles(h_ref.shape[0]):
        x_ref[hb + rows.start:hb + rows.stop, :] = (
            h_ref[rows, :] + gate_m * _dot(mo_ref[rows, :], wm_ref[...]))
    tm = h_ref.shape[0]
    _ffn_kernel(x_ref.at[pl.ds(hb, tm)], x_ref.at[pl.ds(0, hb)], *rest[:-1],
                fc=fc, final_norm=final_norm)


def _ffn_kernel(h_ref, halo_ref, mod_ref, g_ref, wu_ref, wd_ref, cw_ref, cb_ref,
                fg_ref, o_ref, hn_ref, a_ref, b_ref, gt_ref, *, fc, final_norm):
    i = pl.program_id(1)
    tm = h_ref.shape[0]
    hb = halo_ref.shape[0]
    taps, F = cw_ref.shape
    nc = F // fc
    g, shift, scale = g_ref[...], mod_ref[0:1, :], mod_ref[1:2, :]
    x = h_ref[...]
    keep = (i > 0).astype(f32)
    hn_ref[0:hb, :] = (_modulate(halo_ref[...], g, shift, scale) * keep).astype(bf16)
    a_ref[0, 0:hb, :] = _dot(hn_ref[0:hb, :], wu_ref[:, 0:fc])
    for rows in _subtiles(tm):
        hrows = slice(hb + rows.start, hb + rows.stop)
        hn = _modulate(h_ref[rows, :], g, shift, scale).astype(bf16)
        hn_ref[hrows, :] = hn
        a_ref[0, hrows, :] = _dot(hn, wu_ref[:, 0:fc])
        b_ref[0, rows, :] = _dot(hn, wu_ref[:, F:F + fc])

    def up(j, slot):
        cols = slice(j * fc, (j + 1) * fc)
        a_ref[slot] = _dot(hn_ref[...], wu_ref[:, cols])
        b_ref[slot] = _dot(hn_ref[hb:, :], wu_ref[:, F + j * fc:F + (j + 1) * fc])

    def gate(j, slot):
        cols = slice(j * fc, (j + 1) * fc)
        cw = cw_ref[:, cols]
        ac = cb_ref[:, cols]
        for t in range(taps):
            off = hb - (taps - 1) + t
            ac = ac + cw[t:t + 1, :] * a_ref[slot, off:off + tm, :]
        gt_ref[:, cols] = (_silu(ac) * b_ref[slot]).astype(bf16)

    for j in range(nc):
        if j + 1 < nc:
            up(j + 1, (j + 1) % 2)
        gate(j, j % 2)
    out = x + mod_ref[2:3, :] * _dot(gt_ref[...], wd_ref[...])
    if final_norm:
        ms = jnp.mean(out * out, axis=-1, keepdims=True)
        out = out * lax.rsqrt(ms + EPS) * fg_ref[...]
    o_ref[...] = out


def _ffn(h, mod4, l, g, w_up, conv_w, conv_b, w_down, final_g, tm, fc, final_norm):
    B, S, D = h.shape
    F = w_down.shape[0]
    taps = conv_w.shape[0]
    assert F % fc == 0 and F % LANES == 0
    hb = BF16_ROWS
    per = tm // hb
    kern = functools.partial(_ffn_kernel, fc=fc, final_norm=final_norm)
    return pl.pallas_call(
        kern,
        grid=(B, S // tm),
        in_specs=[
            pl.BlockSpec((None, tm, D), lambda b, i: (b, i, 0)),
            pl.BlockSpec((None, hb, D), lambda b, i: (b, jnp.maximum(i * per - 1, 0), 0)),
            _mod_spec(D)(l),
            _resident((1, D)),
            _resident((D, 2 * F)),
            _resident((F, D)),
            _resident((taps, F)),
            _resident((1, F)),
            _resident((1, D)),
        ],
        out_specs=pl.BlockSpec((None, tm, D), lambda b, i: (b, i, 0)),
        out_shape=jax.ShapeDtypeStruct((B, S, D), f32),
        scratch_shapes=[
            pltpu.VMEM((hb + tm, D), bf16),
            pltpu.VMEM((2, hb + tm, fc), f32),
            pltpu.VMEM((2, tm, fc), f32),
            pltpu.VMEM((tm, F), bf16),
        ],
        compiler_params=_cparams("parallel", "parallel"),
        name="ffn",
    )(h, h, mod4, g, w_up.astype(bf16), w_down.astype(bf16), conv_w, conv_b.reshape(1, F),
      final_g)


def _ffn_after_attn(o, h, mod4, l_mix, w_mix_bf, l, g, w_up, conv_w, conv_b, w_down, final_g,
                    fc, final_norm):
    B, S, D = h.shape
    F = w_down.shape[0]
    taps = conv_w.shape[0]
    tm = o.shape[3]
    nq = S // tm
    hb = BF16_ROWS
    per = tm // hb
    assert F % fc == 0 and F % LANES == 0

    def pair_of(q):
        return jnp.where(q < nq // 2, q, nq - 1 - q), (q >= nq // 2).astype(jnp.int32)

    def o_tile(b, i):
        pr, sl = pair_of(i)
        return (b, pr, sl, 0, 0)

    def o_halo(b, i):
        pr, sl = pair_of(jnp.maximum(i - 1, 0))
        return (b, pr, sl, per - 1, 0)

    kern = functools.partial(_ffn_mix_kernel, fc=fc, final_norm=final_norm)
    return pl.pallas_call(
        kern,
        grid=(B, nq),
        in_specs=[
            pl.BlockSpec((None, None, None, tm, D), o_tile),
            pl.BlockSpec((None, None, None, hb, D), o_halo),
            _mod_spec(D)(l_mix),
            _resident((D, D)),
            pl.BlockSpec((None, tm, D), lambda b, i: (b, i, 0)),
            pl.BlockSpec((None, hb, D), lambda b, i: (b, jnp.maximum(i * per - 1, 0), 0)),
            _mod_spec(D)(l),
            _resident((1, D)),
            _resident((D, 2 * F)),
            _resident((F, D)),
            _resident((taps, F)),
            _resident((1, F)),
            _resident((1, D)),
        ],
        out_specs=pl.BlockSpec((None, tm, D), lambda b, i: (b, i, 0)),
        out_shape=jax.ShapeDtypeStruct((B, S, D), f32),
        scratch_shapes=[
            pltpu.VMEM((hb + tm, D), bf16),
            pltpu.VMEM((2, hb + tm, fc), f32),
            pltpu.VMEM((2, tm, fc), f32),
            pltpu.VMEM((tm, F), bf16),
            pltpu.VMEM((hb + tm, D), f32),
        ],
        compiler_params=_cparams("parallel", "parallel"),
        name="ffn_mix",
    )(o, o, mod4, w_mix_bf, h, h, mod4, g, w_up.astype(bf16), w_down.astype(bf16), conv_w,
      conv_b.reshape(1, F), final_g)


LOG2E = 1.4426950408889634


FBIAS_PIECES = 3


def _fbias_selector(n_heads):
    heads = 2
    sel = np.zeros((FBIAS_PIECES * LANES, (n_heads // heads) * LANES), np.float32)
    for h in range(n_heads):
        for p in range(FBIAS_PIECES):
            sel[p * LANES + h, (h // heads) * LANES + (h % heads) * FBIAS_PIECES + p] = 1.0
    return jnp.asarray(sel, dtype=bf16)


def _fox_in_kernel(h_ref, mod_ref, g_ref, wqt_ref, wk_ref, wvt_ref, wf_ref, bf_ref, sel_ref,
                   qt_ref, k_ref, vt_ref, fa_ref, carry_ref, *, q_scale, ts):
    i = pl.program_id(1)
    tm, D = h_ref.shape

    @pl.when(i == 0)
    def _():
        carry_ref[...] = jnp.zeros_like(carry_ref)

    g, shift, scale = g_ref[...], mod_ref[0:1, :], mod_ref[1:2, :]
    subs = _subtiles(tm)
    sub = subs[0].stop
    r = lax.broadcasted_iota(jnp.int32, (sub, sub), 0)
    c = lax.broadcasted_iota(jnp.int32, (sub, sub), 1)
    tri = (c <= r).astype(bf16)
    carry = carry_ref[...]
    for rows in subs:
        hn = _modulate(h_ref[rows, :], g, shift, scale).astype(bf16)
        k_ref[rows, :] = _dot(hn, wk_ref[...]).astype(bf16)
        qt_ref[:, rows] = (_dot_nt(wqt_ref[...], hn) * q_scale).astype(bf16)
        vt = _dot_nt(wvt_ref[...], hn).astype(bf16)
        for cc in range(sub // ts):
            vt_ref[rows.start // ts + cc] = vt[:, cc * ts:(cc + 1) * ts]

        fl = _dot(hn, wf_ref[...]) + bf_ref[...]
        lf = jnp.minimum(fl, 0.0) - jnp.log(1.0 + jnp.exp(-jnp.abs(fl)))
        p0, p1, p2 = _split3(lf)
        cs = _dot(tri, p0) + _dot(tri, p1) + _dot(tri, p2) + carry
        carry = cs[sub - 1:sub, :]
        fa = _dot(jnp.concatenate(_split3(cs * LOG2E), axis=1), sel_ref[...]).astype(bf16)
        for hp in range(fa_ref.shape[0]):
            fa_ref[hp, rows, :] = fa[:, hp * LANES:(hp + 1) * LANES]
    carry_ref[...] = carry


def _fox_in(h, mod4, l, g, wqt_bf, wk_bf, wvt_bf, wf_bf, bf_pad, sel, q_scale, tm, ts):
    B, S, D = h.shape
    nhp = sel.shape[1] // LANES
    return pl.pallas_call(
        functools.partial(_fox_in_kernel, q_scale=q_scale, ts=ts),
        grid=(B, S // tm),
        in_specs=[
            pl.BlockSpec((None, tm, D), lambda b, i: (b, i, 0)),
            _mod_spec(D)(l),
            _resident((1, D)),
            _resident((D, D)),
            _resident((D, D)),
            _resident((D, D)),
            _resident((D, LANES)),
            _resident((1, LANES)),
            _resident(sel.shape),
        ],
        out_specs=[
            pl.BlockSpec((None, D, tm), lambda b, i: (b, 0, i)),
            pl.BlockSpec((None, tm, D), lambda b, i: (b, i, 0)),
            pl.BlockSpec((None, tm // ts, D, ts), lambda b, i: (b, i, 0, 0)),
            pl.BlockSpec((None, nhp, tm, LANES), lambda b, i: (b, 0, i, 0)),
        ],
        out_shape=[
            jax.ShapeDtypeStruct((B, D, S), bf16),
            jax.ShapeDtypeStruct((B, S, D), bf16),
            jax.ShapeDtypeStruct((B, S // ts, D, ts), bf16),
            jax.ShapeDtypeStruct((B, nhp, S, LANES), bf16),
        ],
        scratch_shapes=[pltpu.VMEM((1, LANES), f32)],
        compiler_params=_cparams("parallel", "arbitrary"),
        name="fox_in",
    )(h, mod4, g, wqt_bf, wk_bf, wvt_bf, wf_bf, bf_pad, sel)


NEG_BIG = -1e30
ATTN_PAIR = 2
ATTN_GROUPS_PER_TRIP = 4
ATTN_LANE_SPLIT = 256


def _sublane_allreduce(x, op):
    for sh in (4, 2, 1):
        x = op(x, pltpu.roll(x, sh, axis=0))
    return x


def _tree(xs, op):
    while len(xs) > 1:
        xs = [op(xs[i], xs[i + 1]) if i + 1 < len(xs) else xs[i] for i in range(0, len(xs), 2)]
    return xs[0]


def _attn_kernel(qta_ref, qtb_ref, k_ref, fa_ref, vt_ref, o_ref, s_ref, mx_ref, *, ts, dh):
    i = pl.program_id(2)
    nq = 2 * pl.num_programs(2)
    _attn_block(i, qta_ref, k_ref, fa_ref, vt_ref, o_ref.at[0], s_ref, mx_ref, ts=ts, dh=dh)
    _attn_block(nq - 1 - i, qtb_ref, k_ref, fa_ref, vt_ref, o_ref.at[1], s_ref, mx_ref,
                ts=ts, dh=dh)


def _attn_block(qi, qt_ref, k_ref, fa_ref, vt_ref, o_ref, s_ref, mx_ref, *, ts, dh):
    tq = qt_ref.shape[1]
    heads = LANES // dh
    spq = tq // ts

    qt = qt_ref[...]
    rowi = lax.broadcasted_iota(jnp.int32, qt.shape, 0)
    qts = []
    for e in range(heads):
        own_f = (rowi // FBIAS_PIECES) == e
        qts.append(jnp.concatenate(
            [jnp.where((rowi // dh) == e, qt, jnp.zeros_like(qt)),
             jnp.where(own_f, -1.0, 0.0).astype(bf16)], axis=0))
    nrt = ts // SUBLANES

    def issue(g, bank):
        gk = spq * ts
        ks = pl.ds(pl.multiple_of(g * gk, gk), gk)
        kb = jnp.concatenate([k_ref[ks, :], fa_ref[ks, :]], axis=1)
        for e in range(heads):
            s = _dot(kb, qts[e])
            for d in range(spq):
                sd = s[d * ts:(d + 1) * ts, :]
                s_ref[bank, d, e] = sd
                mx_ref[bank, d, e] = _tree([sd[r * SUBLANES:(r + 1) * SUBLANES, :]
                                            for r in range(nrt)], jnp.maximum)

    ones_rows = (lax.broadcasted_iota(jnp.int32, (BF16_ROWS, ATTN_PAIR * ts), 0) == 0).astype(bf16)
    acc_rows = dh + SUBLANES

    def absorb(g, bank, ds, carry, masked):
        vtb = jnp.concatenate([vt_ref[g * spq + d] for d in ds], axis=1)
        lane0 = ds[0] * ts if masked else 0
        out = []
        for e in range(heads):
            m_all, acc_all = carry[2 * e:2 * e + 2]
            lhs = jnp.concatenate([vtb[e * dh:(e + 1) * dh, :], ones_rows], axis=0)
            if masked:
                mtiles = []
                for d in ds:
                    kidx = lax.broadcasted_iota(jnp.int32, (ts, tq - lane0), 0) + d * ts
                    qidx = lax.broadcasted_iota(jnp.int32, (ts, tq - lane0), 1) + lane0
                    sm = jnp.where(kidx <= qidx, s_ref[bank, d, e, :, lane0:], NEG_BIG)
                    mtiles += [sm[r * SUBLANES:(r + 1) * SUBLANES, :] for r in range(nrt)]
                segs = [(lane0, tq)]
            else:
                segs = [(a, a + ATTN_LANE_SPLIT)
                        for a in range(0, tq, ATTN_LANE_SPLIT)]
            ms = [m_all[:, :lane0]] if lane0 else []
            accs = [acc_all[:, :lane0]] if lane0 else []
            for a, b in segs:
                m, acc = m_all[:, a:b], acc_all[:, a:b]
                if masked:
                    tiles = mtiles
                    mx = _tree(tiles, jnp.maximum)
                else:
                    mx = _tree([mx_ref[bank, d, e, :, a:b] for d in ds], jnp.maximum)
                    tiles = [s_ref[bank, d, e, r * SUBLANES:(r + 1) * SUBLANES, a:b]
                             for d in ds for r in range(nrt)]
                m_new = jnp.maximum(m, _sublane_allreduce(mx, jnp.maximum))
                alpha = jnp.exp2(m - m_new)
                p = jnp.concatenate([jnp.exp2(t - m_new) for t in tiles], axis=0).astype(bf16)
                pv = _dot(lhs, p)[:acc_rows, :]
                ms.append(m_new)
                accs.append(jnp.concatenate([alpha] * (acc_rows // SUBLANES), axis=0) * acc + pv)
            out += [jnp.concatenate(ms, axis=1), jnp.concatenate(accs, axis=1)]
        return tuple(out)

    assert spq % ATTN_PAIR == 0

    def group(g, bank, carry, masked):
        for d0 in range(0, spq, ATTN_PAIR):
            carry = absorb(g, bank, list(range(d0, d0 + ATTN_PAIR)), carry, masked)
        return carry

    init = []
    for e in range(heads):
        init += [jnp.full((SUBLANES, tq), NEG_BIG, f32), jnp.zeros((acc_rows, tq), f32)]
    issue(0, 0)

    gpt = ATTN_GROUPS_PER_TRIP

    def trip(t, c):
        g0 = gpt * t
        for k in range(gpt):
            issue(g0 + k + 1, (k + 1) % 2)
            c = group(g0 + k, k % 2, c, False)
        return c

    carry = lax.fori_loop(0, qi // gpt, trip, tuple(init))
    g0 = qi - qi % gpt

    def tail(r):
        def run(c):
            for k in range(r):
                issue(g0 + k + 1, (k + 1) % 2)
                c = group(g0 + k, k % 2, c, False)
            return group(g0 + r, r % 2, c, True)
        return run

    carry = lax.switch(qi % gpt, [tail(r) for r in range(gpt)], carry)
    outs = []
    for e in range(heads):
        acc = carry[2 * e + 1]
        l = jnp.broadcast_to(acc[dh:dh + 1, :], (dh, tq))
        outs.append(acc[:dh, :] / l)
    o_ref[...] = jnp.concatenate(outs, axis=0).T.astype(o_ref.dtype)


def _attn(qt, k, fa, vt, n_heads, tq):
    B, D, S = qt.shape
    ts = vt.shape[-1]
    dh = D // n_heads
    heads = LANES // dh
    nhp = D // LANES
    assert heads * FBIAS_PIECES <= LANES and heads == 2
    nq = S // tq
    assert nq % 2 == 0
    kern = functools.partial(_attn_kernel, ts=ts, dh=dh)
    return pl.pallas_call(
        kern,
        grid=(B, nhp, nq // 2),
        in_specs=[
            pl.BlockSpec((None, LANES, tq), lambda b, h, i: (b, h, i)),
            pl.BlockSpec((None, LANES, tq), lambda b, h, i: (b, h, nq - 1 - i)),
            pl.BlockSpec((None, S, LANES), lambda b, h, i: (b, 0, h)),
            pl.BlockSpec((None, None, S, LANES), lambda b, h, i: (b, h, 0, 0)),
            pl.BlockSpec((None, S // ts, LANES, ts), lambda b, h, i: (b, 0, h, 0)),
        ],
        out_specs=pl.BlockSpec((None, None, 2, tq, LANES), lambda b, h, i: (b, i, 0, 0, h)),
        out_shape=jax.ShapeDtypeStruct((B, nq // 2, 2, tq, D), bf16),
        scratch_shapes=[pltpu.VMEM((2, tq // ts, heads, ts, tq), f32),
                        pltpu.VMEM((2, tq // ts, heads, SUBLANES, tq), f32)],
        compiler_params=_cparams("parallel", "parallel", "arbitrary"),
        name="attn",
    )(qt, qt, k, fa, vt)


def kernel(x, c, norm_g, ada_w, ada_b, s5_w_in, s5_lam_re, s5_lam_im, s5_log_dt, s5_b_re, s5_b_im, s5_c_re, s5_c_im, s5_d, s5_w_glu, s5_w_out, fox_w_in, fox_b_f, fox_w_out, ffn_w_up, ffn_conv_w, ffn_conv_b, ffn_w_down, final_g):
    B, S, D = x.shape
    depth = norm_g.shape[0]
    n_heads = fox_b_f.shape[-1]
    tm = min(1024, S)
    tmm = min(1024, S)
    fc = 256
    final = final_g.reshape(1, D)

    mod = _adaln(c, ada_w, ada_b)
    mod4 = mod.reshape(depth * 2, B, 3, D)

    h = x
    for i in range(depth):
        j = i // 2
        g_mix = norm_g[i, 0].reshape(1, D)
        g_ffn = norm_g[i, 1].reshape(1, D)
        if i % 2 == 0:
            u = _s5_in(h, mod4, 2 * i, g_mix, s5_w_in[j].astype(bf16), tmm)
            wcs, wout, wtz, adec = _s5_weights(s5_lam_re[j], s5_lam_im[j], s5_log_dt[j],
                                               s5_b_re[j], s5_b_im[j], s5_c_re[j], s5_c_im[j],
                                               S5_CHUNK)
            z = _s5_core(u, wcs, wout, wtz, adec, s5_d[j].reshape(1, D), S5_CHUNK)
            h = _s5_out(z, h, mod4, 2 * i, s5_w_glu[j].astype(bf16),
                        s5_w_out[j].astype(bf16), tmm)
        else:
            w_in = fox_w_in[j]
            wqt = w_in[:, :D].T.astype(bf16)
            wk = w_in[:, D:2 * D].astype(bf16)
            wvt = w_in[:, 2 * D:3 * D].T.astype(bf16)
            wf = jnp.pad(w_in[:, 3 * D:], ((0, 0), (0, LANES - n_heads))).astype(bf16)
            bfp = jnp.pad(fox_b_f[j], (0, LANES - n_heads)).reshape(1, LANES)
            ts = LANES
            q_scale = float((D // n_heads) ** -0.5 * LOG2E)
            qt, k, vt, fa = _fox_in(h, mod4, 2 * i, g_mix, wqt, wk, wvt, wf, bfp,
                                    _fbias_selector(n_heads), q_scale, tmm, ts)
            o = _attn(qt, k, fa, vt, n_heads, tq=min(512, S))
            h = _ffn_after_attn(o, h, mod4, 2 * i, fox_w_out[j].astype(bf16), 2 * i + 1, g_ffn,
                                ffn_w_up[i], ffn_conv_w[i], ffn_conv_b[i], ffn_w_down[i], final,
                                fc, final_norm=(i == depth - 1))
            continue
        h = _ffn(h, mod4, 2 * i + 1, g_ffn, ffn_w_up[i], ffn_conv_w[i], ffn_conv_b[i],
                 ffn_w_down[i], final, tm, fc, final_norm=(i == depth - 1))
    return h
```
